```python
import math
import jax, jax.numpy as jnp
from jax import lax
import numpy as np

D_MODEL = 1024
BATCH = 16
SEQ = 2048
DEPTH = 2

SSD_HEAD_DIM = 64
SSD_HEADS = D_MODEL // SSD_HEAD_DIM
D_SSD = SSD_HEADS * SSD_HEAD_DIM
SSD_GROUPS = 2
SSD_HEADS_PER_GROUP = SSD_HEADS // SSD_GROUPS
D_STATE = 128
SSD_CONV = 4
SSD_CHUNK = 128
CONV_DIM = D_SSD + 2 * SSD_GROUPS * D_STATE
GM_HEAD_DIM = 128
GM_HEADS = D_MODEL // GM_HEAD_DIM
D_GM = GM_HEADS * GM_HEAD_DIM
GM_CHUNK = 128
D_MIX = D_SSD + D_GM
N_IN = D_SSD + CONV_DIM + SSD_HEADS + 2 * D_GM
D_FF = ((8 * D_MODEL // 3 + 255) // 256) * 256
FF_CONV = 3
N_MOD = 6
EPS = 1e-6

kernel_name = "hybrid_ssd_gmlp_convffn_adaln"


def rmsnorm(x, g):
    xf = x.astype(jnp.float32)
    y = xf * lax.rsqrt(jnp.mean(xf * xf, axis=-1, keepdims=True) + EPS)
    return (y * g.astype(jnp.float32)).astype(x.dtype)


def causal_dwconv(x, w, b):
    k, ch = w.shape
    y = lax.conv_general_dilated(x, w[:, None, :].astype(x.dtype), window_strides=(1,), padding=[(k - 1, 0)], dimension_numbers=("NWC", "WIO", "NWC"), feature_group_count=ch)
    return y + b.astype(x.dtype)


def ssd_chunked(xh, dt, a, bm, cm):
    bsz, s = xh.shape[:2]
    nc = s // SSD_CHUNK
    chunk = lambda t: t.reshape((bsz, nc, SSD_CHUNK) + t.shape[2:])
    xc = chunk(xh * dt[..., None])
    bc, cc = chunk(bm), chunk(cm)
    a_cs = jnp.cumsum(jnp.moveaxis(chunk(dt * a), 2, -1), axis=-1)
    causal = jnp.tril(jnp.ones((SSD_CHUNK, SSD_CHUNK), dtype=bool))
    seg = jnp.where(causal, a_cs[..., :, None] - a_cs[..., None, :], -jnp.inf)
    cb = jnp.einsum("bclgn,bcsgn->bcgls", cc, bc)
    att = cb[:, :, :, None] * jnp.exp(seg)
    y_diag = jnp.einsum("bcgrls,bcsgrp->bclgrp", att, xc)
    decay_st = jnp.moveaxis(jnp.exp(a_cs[..., -1:] - a_cs), -1, 2)
    states = jnp.einsum("bclgn,bclgrp->bcgrpn", bc, xc * decay_st[..., None])
    chunk_decay = jnp.exp(a_cs[..., -1])

    def step(h, inp):
        st, dec = inp
        return h * dec[..., None, None] + st, h

    h0 = jnp.zeros_like(states[:, 0])
    _, prev = lax.scan(step, h0, (jnp.moveaxis(states, 1, 0), jnp.moveaxis(chunk_decay, 1, 0)))
    prev = jnp.moveaxis(prev, 0, 1)
    decay_out = jnp.moveaxis(jnp.exp(a_cs), -1, 2)
    y_off = jnp.einsum("bclgn,bcgrpn->bclgrp", cc, prev) * decay_out[..., None]
    return (y_diag + y_off).reshape(xh.shape)


def hybrid_mixer(h, w_in, conv_w, conv_b, dt_bias, a_log, d_skip, ssd_norm_g, v_norm_g, ws, bs, gm_out_g, w_out):
    bsz, s, _ = h.shape
    f32 = jnp.float32
    proj = h @ w_in
    z, xbc, dt_raw, gm = jnp.split(proj, [D_SSD, D_SSD + CONV_DIM, D_SSD + CONV_DIM + SSD_HEADS], axis=-1)
    xbc = jax.nn.silu(causal_dwconv(xbc, conv_w, conv_b))
    xs, bm, cm = jnp.split(xbc, [D_SSD, D_SSD + SSD_GROUPS * D_STATE], axis=-1)
    xs = xs.astype(f32).reshape(bsz, s, SSD_GROUPS, SSD_HEADS_PER_GROUP, SSD_HEAD_DIM)
    bm = bm.astype(f32).reshape(bsz, s, SSD_GROUPS, D_STATE)
    cm = cm.astype(f32).reshape(bsz, s, SSD_GROUPS, D_STATE)
    dt = jax.nn.softplus(dt_raw.astype(f32) + dt_bias.astype(f32)).reshape(bsz, s, SSD_GROUPS, SSD_HEADS_PER_GROUP)
    a = -jnp.exp(a_log.astype(f32)).reshape(SSD_GROUPS, SSD_HEADS_PER_GROUP)
    y = ssd_chunked(xs, dt, a, bm, cm) + d_skip.astype(f32).reshape(SSD_GROUPS, SSD_HEADS_PER_GROUP)[..., None] * xs
    gw = SSD_HEADS_PER_GROUP * SSD_HEAD_DIM
    y = y.reshape(bsz, s, SSD_GROUPS, gw) * jax.nn.silu(z.astype(f32)).reshape(bsz, s, SSD_GROUPS, gw)
    y = rmsnorm(y, ssd_norm_g.reshape(SSD_GROUPS, gw)).reshape(bsz, s, D_SSD).astype(h.dtype)
    u, v = jnp.split(jax.nn.gelu(gm, approximate=False), 2, axis=-1)
    v = rmsnorm(v, v_norm_g).reshape(bsz, s // GM_CHUNK, GM_CHUNK, GM_HEADS, GM_HEAD_DIM)
    ws_c = jnp.where(jnp.tril(jnp.ones((GM_CHUNK, GM_CHUNK), dtype=bool)), ws, jnp.zeros_like(ws))
    sv = jnp.einsum("hts,bcshd->bcthd", ws_c, v) + bs.T[:, :, None]
    g_out = rmsnorm(u * sv.reshape(bsz, s, D_GM), gm_out_g)
    return jnp.concatenate([y, g_out], axis=-1) @ w_out


def conv_gated_ffn(h, w_up, conv_w, conv_b, w_down):
    gate, val = jnp.split(h @ w_up, 2, axis=-1)
    gate = causal_dwconv(gate, conv_w, conv_b)
    return (jax.nn.silu(gate) * val) @ w_down


def setup_inputs(seed: int = 0) -> dict:
    key = jax.random.key(seed)
    ks = jax.random.split(key, 24)
    f32 = jnp.float32
    L = DEPTH
    nrm = lambda k, shape, sc: jax.random.normal(k, shape, f32) * sc
    gain = lambda k, shape: 1.0 + 0.1 * jax.random.normal(k, shape, f32)
    dt0 = jnp.exp(jax.random.uniform(ks[9], (L, SSD_HEADS), f32, math.log(1e-3), math.log(1e-1)))
    return {
        "x": nrm(ks[0], (BATCH, SEQ, D_MODEL), 1.0),
        "c": nrm(ks[1], (BATCH, D_MODEL), 1.0),
        "ada_w": nrm(ks[2], (L, D_MODEL, N_MOD * D_MODEL), 0.5 * D_MODEL ** -0.5),
        "ada_b": nrm(ks[3], (L, N_MOD * D_MODEL), 0.01),
        "norm1_g": gain(ks[4], (L, D_MODEL)),
        "norm2_g": gain(ks[5], (L, D_MODEL)),
        "w_in": nrm(ks[6], (L, D_MODEL, N_IN), D_MODEL ** -0.5),
        "ssd_conv_w": nrm(ks[7], (L, SSD_CONV, CONV_DIM), SSD_CONV ** -0.5),
        "ssd_conv_b": nrm(ks[8], (L, CONV_DIM), 0.01),
        "ssd_dt_bias": dt0 + jnp.log(-jnp.expm1(-dt0)),
        "ssd_a_log": jnp.log(jax.random.uniform(ks[10], (L, SSD_HEADS), f32, 1.0, 16.0)),
        "ssd_d": gain(ks[11], (L, SSD_HEADS)),
        "ssd_norm_g": gain(ks[12], (L, D_SSD)),
        "gm_vnorm_g": gain(ks[13], (L, D_GM)),
        "gm_ws": nrm(ks[14], (L, GM_HEADS, GM_CHUNK, GM_CHUNK), GM_CHUNK ** -0.5),
        "gm_bs": gain(ks[15], (L, GM_HEADS, GM_CHUNK)),
        "gm_out_g": gain(ks[16], (L, D_GM)),
        "w_out": nrm(ks[17], (L, D_MIX, D_MODEL), D_MIX ** -0.5),
        "ff_up": nrm(ks[18], (L, D_MODEL, 2 * D_FF), D_MODEL ** -0.5),
        "ff_conv_w": nrm(ks[19], (L, FF_CONV, D_FF), FF_CONV ** -0.5),
        "ff_conv_b": nrm(ks[20], (L, D_FF), 0.01),
        "ff_down": nrm(ks[21], (L, D_FF, D_MODEL), D_FF ** -0.5),
        "final_g": gain(ks[22], (D_MODEL,)),
    }


def reference(x, c, ada_w, ada_b, norm1_g, norm2_g, w_in, ssd_conv_w, ssd_conv_b, ssd_dt_bias, ssd_a_log, ssd_d, ssd_norm_g, gm_vnorm_g, gm_ws, gm_bs, gm_out_g, w_out, ff_up, ff_conv_w, ff_conv_b, ff_down, final_g):
    c_act = jax.nn.silu(c)
    for l in range(DEPTH):
        mod = (c_act @ ada_w[l] + ada_b[l])[:, None, :]
        sh1, sc1, g1, sh2, sc2, g2 = jnp.split(mod, N_MOD, axis=-1)
        h = rmsnorm(x, norm1_g[l]) * (1 + sc1) + sh1
        x = x + g1 * hybrid_mixer(h, w_in[l], ssd_conv_w[l], ssd_conv_b[l], ssd_dt_bias[l], ssd_a_log[l], ssd_d[l], ssd_norm_g[l], gm_vnorm_g[l], gm_ws[l], gm_bs[l], gm_out_g[l], w_out[l])
        h = rmsnorm(x, norm2_g[l]) * (1 + sc2) + sh2
        x = x + g2 * conv_gated_ffn(h, ff_up[l], ff_conv_w[l], ff_conv_b[l], ff_down[l])
    return rmsnorm(x, final_g)
```

```python
import functools

import jax
import jax.numpy as jnp
from jax import lax
from jax.experimental import pallas as pl
from jax.experimental.pallas import tpu as pltpu

F32 = jnp.float32
BF16 = jnp.bfloat16

D_MODEL = 1024
SSD_HEAD_DIM = 64
SSD_HEADS = 16
D_SSD = 1024
SSD_GROUPS = 2
HEADS_PER_GROUP = 8
D_STATE = 128
SSD_CONV = 4
CHUNK = 128
CONV_DIM = D_SSD + 2 * SSD_GROUPS * D_STATE
GM_HEAD_DIM = 128
GM_HEADS = 8
D_GM = 1024
D_FF = 2816
FF_CONV = 3
N_MOD = 6
EPS = 1e-6

LANES = 128
HIST = 8
TOK = 256
VMEM_LIMIT = 56 * 1024 * 1024


def _dot(a, b):
    return jnp.dot(a, b, preferred_element_type=F32)


def _silu(x):
    return x / (1.0 + jnp.exp(-x))


def _rms_scale(x):
    return lax.rsqrt(jnp.mean(x * x, axis=-1, keepdims=True) + EPS)


def _split3(x):
    hi = x.astype(BF16)
    r1 = x - hi.astype(F32)
    mid = r1.astype(BF16)
    lo = (r1 - mid.astype(F32)).astype(BF16)
    return hi, mid, lo


def _ada_kernel(c_ref, w_ref, b_ref, o_ref):
    c = c_ref[...]
    ca = _silu(c).astype(BF16)
    o_ref[0] = _dot(ca, w_ref[0].astype(BF16)) + b_ref[0]


def _ada(c, ada_w, ada_b):
    n_layers, _, n_out = ada_w.shape
    bsz = c.shape[0]
    tn = 1536
    return pl.pallas_call(
        _ada_kernel,
        grid=(n_layers, n_out // tn),
        in_specs=[
            pl.BlockSpec((bsz, D_MODEL), lambda l, j: (0, 0)),
            pl.BlockSpec((1, D_MODEL, tn), lambda l, j: (l, 0, j)),
            pl.BlockSpec((1, 1, tn), lambda l, j: (l, 0, j)),
        ],
        out_specs=pl.BlockSpec((1, bsz, tn), lambda l, j: (l, 0, j)),
        out_shape=jax.ShapeDtypeStruct((n_layers, bsz, n_out), F32),
        compiler_params=pltpu.CompilerParams(
            dimension_semantics=("arbitrary", "arbitrary"),
            vmem_limit_bytes=VMEM_LIMIT),
        name="ada_mod",
    )(c, ada_w, ada_b.reshape(n_layers, 1, n_out))


def _mixer_kernel(x_ref, mod_ref, n1g_ref, wz_ref, wxbc_ref, wdt_ref, wu_ref,
                  wv_ref, cw_ref, cb_ref, dtb_ref, alog_ref, dsk_ref, ssdg_ref,
                  vng_ref, ws_ref, bsx_ref, gog_ref, wo1_ref, wo2_ref, o_ref,
                  xbc_scr, state_scr, y_scr):
    tok = x_ref.shape[1]

    @pl.when(pl.program_id(1) == 0)
    def _():
        xbc_scr[0:HIST, :] = jnp.zeros((HIST, CONV_DIM), F32)
        state_scr[...] = jnp.zeros(state_scr.shape, F32)

    x = x_ref[0]
    mod = mod_ref[0]
    sh1, sc1, g1 = mod[0:1], mod[1:2], mod[2:3]
    h = (x * _rms_scale(x)) * (n1g_ref[...] * (1.0 + sc1)) + sh1
    hb = h.astype(BF16)

    z = _dot(hb, wz_ref[...])
    xbc_scr[HIST:HIST + tok, :] = _dot(hb, wxbc_ref[...])
    dt_raw = _dot(hb, wdt_ref[...])
    u_raw = _dot(hb, wu_ref[...])
    v_raw = _dot(hb, wv_ref[...])

    acc = cb_ref[...]
    for k in range(SSD_CONV):
        start = HIST - (SSD_CONV - 1) + k
        acc = acc + cw_ref[k:k + 1, :] * xbc_scr[start:start + tok, :]
    xbc_scr[0:HIST, :] = xbc_scr[tok:tok + HIST, :]
    xbc = _silu(acc)
    xs = xbc[:, :D_SSD]
    xs_b = xs.astype(BF16)
    bmat = xbc[:, D_SSD:D_SSD + SSD_GROUPS * D_STATE]
    cmat = xbc[:, D_SSD + SSD_GROUPS * D_STATE:]

    lane = lax.broadcasted_iota(jnp.int32, (1, LANES), 1)
    a_row = jnp.where(lane < SSD_HEADS, -jnp.exp(alog_ref[...]), 0.0)
    dt_in = dt_raw + dtb_ref[...]
    dt = jnp.maximum(dt_in, 0.0) + jnp.log1p(jnp.exp(-jnp.abs(dt_in)))
    dta = dt * a_row

    li = lax.broadcasted_iota(jnp.int32, (CHUNK, CHUNK), 0)
    si = lax.broadcasted_iota(jnp.int32, (CHUNK, CHUNK), 1)
    causal = li >= si
    tril_b = jnp.where(causal, 1.0, 0.0).astype(BF16)

    for c in range(tok // CHUNK):
        r0 = c * CHUNK
        hi, mid, lo = _split3(dta[r0:r0 + CHUNK])
        a_cs = _dot(tril_b, hi) + _dot(tril_b, mid) + _dot(tril_b, lo)
        a_cs_t = a_cs.T
        dt_t = dt[r0:r0 + CHUNK].T
        a_last = a_cs_t[:, CHUNK - 1:CHUNK]
        w_rows = dt_t * jnp.exp(a_last - a_cs_t)
        xs_c = xs_b[r0:r0 + CHUNK]
        cb_g, bt_g, c_g = [], [], []
        for g in range(SSD_GROUPS):
            b_blk = bmat[r0:r0 + CHUNK, g * D_STATE:(g + 1) * D_STATE]
            c_blk = cmat[r0:r0 + CHUNK, g * D_STATE:(g + 1) * D_STATE]
            cb_g.append(lax.dot_general(
                c_blk.astype(BF16), b_blk.astype(BF16),
                (((1,), (1,)), ((), ())), preferred_element_type=F32))
            bt_g.append(b_blk.T)
            c_g.append(c_blk)
        for hd in range(SSD_HEADS):
            g = hd // HEADS_PER_GROUP
            col = jnp.broadcast_to(a_cs[:, hd:hd + 1], (CHUNK, CHUNK))
            seg = jnp.where(causal, col - a_cs_t[hd:hd + 1, :], -jnp.inf)
            att = cb_g[g] * jnp.exp(seg) * dt_t[hd:hd + 1, :]
            cd = c_g[g] * jnp.exp(col)
            lhs = jnp.concatenate([att.astype(BF16), cd.astype(BF16)], axis=1)
            xs_h = xs_c[:, hd * SSD_HEAD_DIM:(hd + 1) * SSD_HEAD_DIM]
            prev = state_scr[hd]
            rhs = jnp.concatenate([xs_h, prev.astype(BF16)], axis=0)
            y_scr[r0:r0 + CHUNK, hd * SSD_HEAD_DIM:(hd + 1) * SSD_HEAD_DIM] = (
                _dot(lhs, rhs))
            bw = (bt_g[g] * w_rows[hd:hd + 1, :]).astype(BF16)
            decay = jnp.exp(col[CHUNK - 1:CHUNK, 0:SSD_HEAD_DIM])
            state_scr[hd] = prev * decay + _dot(bw, xs_h)

    y = (y_scr[...] + dsk_ref[...] * xs) * _silu(z)
    gw = HEADS_PER_GROUP * SSD_HEAD_DIM
    parts = []
    for g in range(SSD_GROUPS):
        yg = y[:, g * gw:(g + 1) * gw]
        parts.append((yg * _rms_scale(yg)) * ssdg_ref[:, g * gw:(g + 1) * gw])
    yn = jnp.concatenate(parts, axis=1).astype(BF16)

    inv_sqrt2 = 0.7071067811865476
    u = 0.5 * u_raw * (1.0 + lax.erf(u_raw * inv_sqrt2))
    v = 0.5 * v_raw * (1.0 + lax.erf(v_raw * inv_sqrt2))
    vn = ((v * _rms_scale(v)) * vng_ref[...]).astype(BF16)
    ws_m = [jnp.where(causal, ws_ref[hd], 0.0).astype(BF16)
            for hd in range(GM_HEADS)]
    rows = []
    for c in range(tok // CHUNK):
        r0 = c * CHUNK
        cols = [_dot(ws_m[hd], vn[r0:r0 + CHUNK,
                                  hd * GM_HEAD_DIM:(hd + 1) * GM_HEAD_DIM])
                for hd in range(GM_HEADS)]
        rows.append(jnp.concatenate(cols, axis=1) + bsx_ref[...])
    sv = jnp.concatenate(rows, axis=0)
    gl = u * sv
    gn = ((gl * _rms_scale(gl)) * gog_ref[...]).astype(BF16)

    out = _dot(yn, wo1_ref[...]) + _dot(gn, wo2_ref[...])
    o_ref[0] = x + g1 * out


def _const_spec(shape):
    zeros = (0,) * len(shape)
    return pl.BlockSpec(shape, lambda b, s: zeros, pipeline_mode=pl.Buffered(1))


def _mixer(x, mod, p):
    bsz, seq, _ = x.shape
    consts = [p["n1g"], p["wz"], p["wxbc"], p["wdt"], p["wu"], p["wv"],
              p["cw"], p["cb"], p["dtb"], p["alog"], p["dsk"], p["ssdg"],
              p["vng"], p["ws"], p["bsx"], p["gog"], p["wo1"], p["wo2"]]
    return pl.pallas_call(
        _mixer_kernel,
        grid=(bsz, seq // TOK),
        in_specs=[
            pl.BlockSpec((1, TOK, D_MODEL), lambda b, s: (b, s, 0)),
            pl.BlockSpec((1, N_MOD, D_MODEL), lambda b, s: (b, 0, 0)),
        ] + [_const_spec(a.shape) for a in consts],
        out_specs=pl.BlockSpec((1, TOK, D_MODEL), lambda b, s: (b, s, 0)),
        out_shape=jax.ShapeDtypeStruct(x.shape, F32),
        scratch_shapes=[
            pltpu.VMEM((HIST + TOK, CONV_DIM), F32),
            pltpu.VMEM((SSD_HEADS, D_STATE, SSD_HEAD_DIM), F32),
            pltpu.VMEM((TOK, D_SSD), F32),
        ],
        compiler_params=pltpu.CompilerParams(
            dimension_semantics=("arbitrary", "arbitrary"),
            vmem_limit_bytes=VMEM_LIMIT),
        name="mixer",
    )(x, mod, *consts)


def _ffn_kernel(x_ref, mod_ref, n2g_ref, wg_ref, wv_ref, cw_ref, cb_ref,
                wd_ref, fg_ref, o_ref, gate_scr, *, final):
    tok = x_ref.shape[1]

    @pl.when(pl.program_id(1) == 0)
    def _():
        gate_scr[0:HIST, :] = jnp.zeros((HIST, D_FF), F32)

    x = x_ref[0]
    mod = mod_ref[0]
    sh2, sc2, g2 = mod[3:4], mod[4:5], mod[5:6]
    h = (x * _rms_scale(x)) * (n2g_ref[...] * (1.0 + sc2)) + sh2
    hb = h.astype(BF16)
    gate_scr[HIST:HIST + tok, :] = _dot(hb, wg_ref[...])
    val = _dot(hb, wv_ref[...])
    acc = cb_ref[...]
    for k in range(FF_CONV):
        start = HIST - (FF_CONV - 1) + k
        acc = acc + cw_ref[k:k + 1, :] * gate_scr[start:start + tok, :]
    gate_scr[0:HIST, :] = gate_scr[tok:tok + HIST, :]
    act = (_silu(acc) * val).astype(BF16)
    xo = x + g2 * _dot(act, wd_ref[...])
    if final:
        xo = (xo * _rms_scale(xo)) * fg_ref[...]
    o_ref[0] = xo


def _ffn(x, mod, p, final_g, final):
    bsz, seq, _ = x.shape
    consts = [p["n2g"], p["wg"], p["wval"], p["fcw"], p["fcb"], p["wd"], final_g]
    return pl.pallas_call(
        functools.partial(_ffn_kernel, final=final),
        grid=(bsz, seq // TOK),
        in_specs=[
            pl.BlockSpec((1, TOK, D_MODEL), lambda b, s: (b, s, 0)),
            pl.BlockSpec((1, N_MOD, D_MODEL), lambda b, s: (b, 0, 0)),
        ] + [_const_spec(a.shape) for a in consts],
        out_specs=pl.BlockSpec((1, TOK, D_MODEL), lambda b, s: (b, s, 0)),
        out_shape=jax.ShapeDtypeStruct(x.shape, F32),
        scratch_shapes=[pltpu.VMEM((HIST + TOK, D_FF), F32)],
        compiler_params=pltpu.CompilerParams(
            dimension_semantics=("arbitrary", "arbitrary"),
            vmem_limit_bytes=VMEM_LIMIT),
        name="ffn_final" if final else "ffn",
    )(x, mod, *consts)


def _pad_lanes(a):
    return jnp.pad(a, ((0, 0), (0, LANES - a.shape[1])))


def _layer_params(l, norm1_g, norm2_g, w_in, ssd_conv_w, ssd_conv_b,
                  ssd_dt_bias, ssd_a_log, ssd_d, ssd_norm_g, gm_vnorm_g, gm_ws,
                  gm_bs, gm_out_g, w_out, ff_up, ff_conv_w, ff_conv_b, ff_down):
    o_xbc = D_SSD
    o_dt = o_xbc + CONV_DIM
    o_u = o_dt + SSD_HEADS
    o_v = o_u + D_GM
    w = w_in[l]
    row = lambda a: a.reshape(1, -1)
    return {
        "n1g": row(norm1_g[l]),
        "wz": w[:, :o_xbc].astype(BF16),
        "wxbc": w[:, o_xbc:o_dt].astype(BF16),
        "wdt": _pad_lanes(w[:, o_dt:o_u]).astype(BF16),
        "wu": w[:, o_u:o_v].astype(BF16),
        "wv": w[:, o_v:].astype(BF16),
        "cw": ssd_conv_w[l],
        "cb": row(ssd_conv_b[l]),
        "dtb": _pad_lanes(row(ssd_dt_bias[l])),
        "alog": _pad_lanes(row(ssd_a_log[l])),
        "dsk": row(jnp.repeat(ssd_d[l], SSD_HEAD_DIM)),
        "ssdg": row(ssd_norm_g[l]),
        "vng": row(gm_vnorm_g[l]),
        "ws": gm_ws[l],
        "bsx": jnp.repeat(gm_bs[l].T, GM_HEAD_DIM, axis=1),
        "gog": row(gm_out_g[l]),
        "wo1": w_out[l][:D_SSD].astype(BF16),
        "wo2": w_out[l][D_SSD:].astype(BF16),
        "n2g": row(norm2_g[l]),
        "wg": ff_up[l][:, :D_FF].astype(BF16),
        "wval": ff_up[l][:, D_FF:].astype(BF16),
        "fcw": ff_conv_w[l],
        "fcb": row(ff_conv_b[l]),
        "wd": ff_down[l].astype(BF16),
    }


def kernel(x, c, ada_w, ada_b, norm1_g, norm2_g, w_in, ssd_conv_w, ssd_conv_b, ssd_dt_bias, ssd_a_log, ssd_d, ssd_norm_g, gm_vnorm_g, gm_ws, gm_bs, gm_out_g, w_out, ff_up, ff_conv_w, ff_conv_b, ff_down, final_g):
    depth = ada_w.shape[0]
    bsz = x.shape[0]
    mod = _ada(c, ada_w, ada_b).reshape(depth, bsz, N_MOD, D_MODEL)
    fg = final_g.reshape(1, D_MODEL)
    for l in range(depth):
        p = _layer_params(l, norm1_g, norm2_g, w_in, ssd_conv_w, ssd_conv_b,
                          ssd_dt_bias, ssd_a_log, ssd_d, ssd_norm_g,
                          gm_vnorm_g, gm_ws, gm_bs, gm_out_g, w_out, ff_up,
                          ff_conv_w, ff_conv_b, ff_down)
        x = _mixer(x, mod[l], p)
        x = _ffn(x, mod[l], p, fg, final=(l == depth - 1))
    return x
```

```python
import functools

import jax
import jax.numpy as jnp
from jax import lax
from jax.experimental import pallas as pl
from jax.experimental.pallas import tpu as pltpu

F32 = jnp.float32
BF16 = jnp.bfloat16

D_MODEL = 1024
SSD_HEAD_DIM = 64
SSD_HEADS = 16
D_SSD = 1024
SSD_GROUPS = 2
HEADS_PER_GROUP = 8
D_STATE = 128
SSD_CONV = 4
CHUNK = 128
CONV_DIM = D_SSD + 2 * SSD_GROUPS * D_STATE
GM_HEAD_DIM = 128
GM_HEADS = 8
D_GM = 1024
D_FF = 2816
FF_CONV = 3
N_MOD = 6
EPS = 1e-6
LOG2E = 1.4426950408889634

LANES = 128
HIST = 8
TOK = 256
VMEM_LIMIT = 56 * 1024 * 1024
GROUP_W = HEADS_PER_GROUP * SSD_HEAD_DIM


def _dot(a, b):
    return jnp.dot(a, b, preferred_element_type=F32)


def _dot_nt(a, b):
    return lax.dot_general(a, b, (((1,), (1,)), ((), ())),
                           preferred_element_type=F32)


def _silu(x):
    half = 0.5 * x
    return half + half * jnp.tanh(half)


def _gelu(x):
    return 0.5 * x * (1.0 + lax.erf(x * 0.7071067811865476))


def _rms_scale(x):
    return lax.rsqrt(jnp.mean(x * x, axis=-1, keepdims=True) + EPS)


def _split3(x):
    hi = x.astype(BF16)
    r1 = x - hi.astype(F32)
    mid = r1.astype(BF16)
    lo = (r1 - mid.astype(F32)).astype(BF16)
    return hi, mid, lo


def _causal_conv(scr, tok, w_ref, b_ref, width):
    full = scr[0:HIST + tok, :]
    acc = b_ref[...] + w_ref[width - 1:width, :] * full[HIST:]
    for j in range(1, width):
        acc = acc + w_ref[width - 1 - j:width - j, :] * pltpu.roll(full, j, 0)[HIST:]
    return acc


def _ada_kernel(c_ref, w_ref, b_ref, o_ref):
    ca = _silu(c_ref[...]).astype(BF16)
    o_ref[0] = _dot(ca, w_ref[0].astype(BF16)) + b_ref[0]


def _ada(c, ada_w, ada_b):
    n_layers, _, n_out = ada_w.shape
    bsz = c.shape[0]
    tn = 1536
    return pl.pallas_call(
        _ada_kernel,
        grid=(n_layers, n_out // tn),
        in_specs=[
            pl.BlockSpec((bsz, D_MODEL), lambda l, j: (0, 0)),
            pl.BlockSpec((1, D_MODEL, tn), lambda l, j: (l, 0, j)),
            pl.BlockSpec((1, 1, tn), lambda l, j: (l, 0, j)),
        ],
        out_specs=pl.BlockSpec((1, bsz, tn), lambda l, j: (l, 0, j)),
        out_shape=jax.ShapeDtypeStruct((n_layers, bsz, n_out), F32),
        compiler_params=pltpu.CompilerParams(
            dimension_semantics=("arbitrary", "arbitrary"),
            vmem_limit_bytes=VMEM_LIMIT),
        name="ada_mod",
    )(c, ada_w, ada_b.reshape(n_layers, 1, n_out))


_PROJ_FIELDS = (("x", None, D_MODEL, F32), ("z", None, D_SSD, F32),
                ("xs", None, D_SSD, F32),
                ("bc", None, 2 * SSD_GROUPS * D_STATE, F32),
                ("u", None, D_GM, F32), ("vn", None, D_GM, BF16),
                ("dt", SSD_HEADS, TOK, F32))
_SCAN_FIELDS = (("x", None, D_MODEL, F32), ("yg", None, D_SSD, F32),
                ("gl", None, D_GM, F32))


def _project_stage(x_ref, mod_ref, w, xbc_scr, slot):
    tok = x_ref.shape[1]
    x = x_ref[0]
    mod = mod_ref[0]
    sh1, sc1 = mod[0:1], mod[1:2]
    h = (x * _rms_scale(x)) * (w["n1g"][...] * (1.0 + sc1)) + sh1
    hb = h.astype(BF16)
    slot["x"][...] = x
    slot["z"][...] = _dot(hb, w["wz"][...])
    xbc_scr[HIST:HIST + tok, :] = _dot(hb, w["wxbc"][...])
    slot["dt"][...] = _dot_nt(w["wdtt"][...], hb)
    u_raw = _dot(hb, w["wu"][...])
    v_raw = _dot(hb, w["wv"][...])
    acc = _causal_conv(xbc_scr, tok, w["cw"], w["cb"], SSD_CONV)
    xbc_scr[0:HIST, :] = xbc_scr[tok:tok + HIST, :]
    xbc = _silu(acc)
    slot["xs"][...] = xbc[:, :D_SSD]
    slot["bc"][...] = xbc[:, D_SSD:]
    slot["u"][...] = _gelu(u_raw)
    v = _gelu(v_raw)
    slot["vn"][...] = ((v * _rms_scale(v)) * w["vng"][...]).astype(BF16)


def _scan_stage(slot, w, state_scr, out_slot):
    tok = slot["xs"].shape[0]
    n_chunks = tok // CHUNK
    xs = slot["xs"][...]
    xs_b = xs.astype(BF16)
    bc = slot["bc"][...]
    bmat = bc[:, :SSD_GROUPS * D_STATE]
    cmat = bc[:, SSD_GROUPS * D_STATE:]

    dt_in = slot["dt"][...] + w["dtb"][...]
    dt_t = jnp.maximum(dt_in, 0.0) + jnp.log1p(jnp.exp(-jnp.abs(dt_in)))
    dta_t = dt_t * (-jnp.exp(w["alog"][...]))

    li = lax.broadcasted_iota(jnp.int32, (CHUNK, CHUNK), 0)
    si = lax.broadcasted_iota(jnp.int32, (CHUNK, CHUNK), 1)
    causal = li >= si
    triu_b = jnp.where(li <= si, 1.0, 0.0).astype(BF16)
    low_half = si < SSD_HEAD_DIM

    y_rows = []
    for c in range(n_chunks):
        r0 = c * CHUNK
        hi, mid, lo = _split3(dta_t[:, r0:r0 + CHUNK])
        a2_t = (_dot(hi, triu_b) + _dot(mid, triu_b) + _dot(lo, triu_b)) * LOG2E
        dt_c = dt_t[:, r0:r0 + CHUNK]
        w_t = dt_c * jnp.exp2(a2_t[:, CHUNK - 1:CHUNK] - a2_t)
        xs_c = xs[r0:r0 + CHUNK]
        xs_cb = xs_b[r0:r0 + CHUNK]
        y_parts = []
        for g in range(SSD_GROUPS):
            b_blk = bmat[r0:r0 + CHUNK, g * D_STATE:(g + 1) * D_STATE]
            c_blk = cmat[r0:r0 + CHUNK, g * D_STATE:(g + 1) * D_STATE]
            cb = _dot_nt(c_blk.astype(BF16), b_blk.astype(BF16))
            bt_b = b_blk.T.astype(BF16)
            state = state_scr[g]
            rhs_g = jnp.concatenate(
                [xs_cb[:, g * GROUP_W:(g + 1) * GROUP_W], state.astype(BF16)],
                axis=0)
            w_parts, dec_parts = [], []
            for j in range(HEADS_PER_GROUP // 2):
                rhs = rhs_g[:, j * LANES:(j + 1) * LANES]
                y_pair, w_pair, e_pair = [], [], []
                for hd in (g * HEADS_PER_GROUP + 2 * j,
                           g * HEADS_PER_GROUP + 2 * j + 1):
                    col = jnp.broadcast_to(a2_t[hd:hd + 1, :], (CHUNK, CHUNK)).T
                    seg = jnp.where(causal, col - a2_t[hd:hd + 1, :], -jnp.inf)
                    att = cb * jnp.exp2(seg) * dt_c[hd:hd + 1, :]
                    e_col = jnp.exp2(col)
                    cd = c_blk * e_col
                    lhs = jnp.concatenate(
                        [att.astype(BF16), cd.astype(BF16)], axis=1)
                    y_pair.append(_dot(lhs, rhs))
                    w_pair.append(jnp.broadcast_to(
                        w_t[hd:hd + 1, :], (SSD_HEAD_DIM, CHUNK)))
                    e_pair.append(e_col[CHUNK - 1:CHUNK, :])
                y_parts.append(jnp.where(low_half, y_pair[0], y_pair[1]))
                w_parts.append(jnp.concatenate(w_pair, axis=0).T)
                dec_parts.append(jnp.where(low_half[0:1], e_pair[0], e_pair[1]))
            w_exp = jnp.concatenate(w_parts, axis=1)
            dec = jnp.concatenate(dec_parts, axis=1)
            xw = (xs_c[:, g * GROUP_W:(g + 1) * GROUP_W] * w_exp).astype(BF16)
            state_scr[g] = state * dec + _dot(bt_b, xw)
        y_rows.append(jnp.concatenate(y_parts, axis=1))
    y_ssd = jnp.concatenate(y_rows, axis=0)

    out_slot["yg"][...] = (y_ssd + w["dsk"][...] * xs) * _silu(slot["z"][...])

    vn = slot["vn"][...]
    sv_heads = []
    for hd in range(GM_HEADS):
        ws_m = jnp.where(causal, w["ws"][hd], 0.0).astype(BF16)
        v_h = jnp.concatenate(
            [vn[c * CHUNK:(c + 1) * CHUNK,
                hd * GM_HEAD_DIM:(hd + 1) * GM_HEAD_DIM]
             for c in range(n_chunks)], axis=1)
        sv_heads.append(_dot(ws_m, v_h))
    sv = jnp.concatenate(
        [jnp.concatenate([s[:, c * CHUNK:(c + 1) * CHUNK] for s in sv_heads],
                         axis=1) + w["bsx"][...]
         for c in range(n_chunks)], axis=0)
    out_slot["gl"][...] = slot["u"][...] * sv
    out_slot["x"][...] = slot["x"][...]


def _out_stage(slot, mod_ref, w, o_ref):
    yg_all = slot["yg"][...]
    parts = []
    for g in range(SSD_GROUPS):
        yg = yg_all[:, g * GROUP_W:(g + 1) * GROUP_W]
        parts.append((yg * _rms_scale(yg))
                     * w["ssdg"][:, g * GROUP_W:(g + 1) * GROUP_W])
    yn = jnp.concatenate(parts, axis=1).astype(BF16)
    gl = slot["gl"][...]
    gn = ((gl * _rms_scale(gl)) * w["gog"][...]).astype(BF16)
    out = _dot(yn, w["wo1"][...]) + _dot(gn, w["wo2"][...])
    g1 = mod_ref[0][2:3]
    o_ref[0] = slot["x"][...] + g1 * out


_MIXER_CONSTS = ("n1g", "wz", "wxbc", "wdtt", "wu", "wv", "cw", "cb", "dtb",
                 "alog", "dsk", "ssdg", "vng", "ws", "bsx", "gog", "wo1", "wo2")


def _slot_dicts(refs, fields):
    n = len(fields)
    return [dict(zip([f[0] for f in fields], refs[i * n:(i + 1) * n]))
            for i in range(2)]


def _mixer_kernel(*refs, tiles_per_seq):
    n_const = len(_MIXER_CONSTS)
    x_ref, mod_in_ref, mod_out_ref = refs[:3]
    w = dict(zip(_MIXER_CONSTS, refs[3:3 + n_const]))
    o_ref = refs[3 + n_const]
    xbc_scr, state_scr = refs[4 + n_const:6 + n_const]
    n_proj = 2 * len(_PROJ_FIELDS)
    proj_slots = _slot_dicts(refs[6 + n_const:6 + n_const + n_proj], _PROJ_FIELDS)
    scan_slots = _slot_dicts(refs[6 + n_const + n_proj:], _SCAN_FIELDS)
    k = pl.program_id(0)

    @pl.when(k == 0)
    def _():
        for ref in list(proj_slots[1].values()) + list(scan_slots[0].values()):
            ref[...] = jnp.zeros(ref.shape, ref.dtype)

    @pl.when(k % tiles_per_seq == 0)
    def _():
        xbc_scr[0:HIST, :] = jnp.zeros((HIST, CONV_DIM), F32)

    @pl.when(jnp.logical_or(k == 0, (k - 1) % tiles_per_seq == 0))
    def _():
        state_scr[...] = jnp.zeros(state_scr.shape, F32)

    def step(p):
        _out_stage(scan_slots[p], mod_out_ref, w, o_ref)
        _scan_stage(proj_slots[1 - p], w, state_scr, scan_slots[1 - p])
        _project_stage(x_ref, mod_in_ref, w, xbc_scr, proj_slots[p])

    @pl.when(k % 2 == 0)
    def _():
        step(0)

    @pl.when(k % 2 == 1)
    def _():
        step(1)


def _const_spec(shape):
    zeros = (0,) * len(shape)
    return pl.BlockSpec(shape, lambda *_: zeros, pipeline_mode=pl.Buffered(1))


def _mixer(x, mod, p):
    bsz, seq, _ = x.shape
    tps = seq // TOK
    n_tiles = bsz * tps
    consts = [p[name] for name in _MIXER_CONSTS]

    def in_tile(k):
        t = jnp.minimum(k, n_tiles - 1)
        return t // tps, t % tps

    def out_tile(k):
        t = jnp.maximum(k - 2, 0)
        return t // tps, t % tps

    def slot_shapes(fields):
        return [pltpu.VMEM((TOK if rows is None else rows, lanes), dt)
                for _, rows, lanes, dt in fields]

    return pl.pallas_call(
        functools.partial(_mixer_kernel, tiles_per_seq=tps),
        grid=(n_tiles + 2,),
        in_specs=[
            pl.BlockSpec((1, TOK, D_MODEL), lambda k: (*in_tile(k), 0)),
            pl.BlockSpec((1, N_MOD, D_MODEL), lambda k: (in_tile(k)[0], 0, 0)),
            pl.BlockSpec((1, N_MOD, D_MODEL), lambda k: (out_tile(k)[0], 0, 0)),
        ] + [_const_spec(a.shape) for a in consts],
        out_specs=pl.BlockSpec((1, TOK, D_MODEL), lambda k: (*out_tile(k), 0)),
        out_shape=jax.ShapeDtypeStruct(x.shape, F32),
        scratch_shapes=[
            pltpu.VMEM((HIST + TOK, CONV_DIM), F32),
            pltpu.VMEM((SSD_GROUPS, D_STATE, GROUP_W), F32),
        ] + 2 * slot_shapes(_PROJ_FIELDS) + 2 * slot_shapes(_SCAN_FIELDS),
        compiler_params=pltpu.CompilerParams(
            dimension_semantics=("arbitrary",),
            vmem_limit_bytes=VMEM_LIMIT),
        name="mixer",
    )(x, mod, mod, *consts)


def _ffn_kernel(x_ref, mod_ref, n2g_ref, wg_ref, wv_ref, cw_ref, cb_ref,
                wd_ref, fg_ref, o_ref, gate_scr, *, final):
    tok = x_ref.shape[1]

    @pl.when(pl.program_id(1) == 0)
    def _():
        gate_scr[0:HIST, :] = jnp.zeros((HIST, D_FF), F32)

    x = x_ref[0]
    mod = mod_ref[0]
    sh2, sc2, g2 = mod[3:4], mod[4:5], mod[5:6]
    h = (x * _rms_scale(x)) * (n2g_ref[...] * (1.0 + sc2)) + sh2
    hb = h.astype(BF16)
    gate_scr[HIST:HIST + tok, :] = _dot(hb, wg_ref[...])
    val = _dot(hb, wv_ref[...])
    acc = _causal_conv(gate_scr, tok, cw_ref, cb_ref, FF_CONV)
    gate_scr[0:HIST, :] = gate_scr[tok:tok + HIST, :]
    act = (_silu(acc) * val).astype(BF16)
    xo = x + g2 * _dot(act, wd_ref[...])
    if final:
        xo = (xo * _rms_scale(xo)) * fg_ref[...]
    o_ref[0] = xo


def _ffn(x, mod, p, final_g, final):
    bsz, seq, _ = x.shape
    consts = [p["n2g"], p["wg"], p["wval"], p["fcw"], p["fcb"], p["wd"], final_g]
    return pl.pallas_call(
        functools.partial(_ffn_kernel, final=final),
        grid=(bsz, seq // TOK),
        in_specs=[
            pl.BlockSpec((1, TOK, D_MODEL), lambda b, s: (b, s, 0)),
            pl.BlockSpec((1, N_MOD, D_MODEL), lambda b, s: (b, 0, 0)),
        ] + [_const_spec(a.shape) for a in consts],
        out_specs=pl.BlockSpec((1, TOK, D_MODEL), lambda b, s: (b, s, 0)),
        out_shape=jax.ShapeDtypeStruct(x.shape, F32),
        scratch_shapes=[pltpu.VMEM((HIST + TOK, D_FF), F32)],
        compiler_params=pltpu.CompilerParams(
            dimension_semantics=("arbitrary", "arbitrary"),
            vmem_limit_bytes=VMEM_LIMIT),
        name="ffn_final" if final else "ffn",
    )(x, mod, *consts)


def _layer_params(l, norm1_g, norm2_g, w_in, ssd_conv_w, ssd_conv_b,
                  ssd_dt_bias, ssd_a_log, ssd_d, ssd_norm_g, gm_vnorm_g, gm_ws,
                  gm_bs, gm_out_g, w_out, ff_up, ff_conv_w, ff_conv_b, ff_down):
    o_xbc = D_SSD
    o_dt = o_xbc + CONV_DIM
    o_u = o_dt + SSD_HEADS
    o_v = o_u + D_GM
    w = w_in[l]
    row = lambda a: a.reshape(1, -1)
    col = lambda a: a.reshape(-1, 1)
    return {
        "n1g": row(norm1_g[l]),
        "wz": w[:, :o_xbc].astype(BF16),
        "wxbc": w[:, o_xbc:o_dt].astype(BF16),
        "wdtt": w[:, o_dt:o_u].T.astype(BF16),
        "wu": w[:, o_u:o_v].astype(BF16),
        "wv": w[:, o_v:].astype(BF16),
        "cw": ssd_conv_w[l],
        "cb": row(ssd_conv_b[l]),
        "dtb": col(ssd_dt_bias[l]),
        "alog": col(ssd_a_log[l]),
        "dsk": row(jnp.repeat(ssd_d[l], SSD_HEAD_DIM)),
        "ssdg": row(ssd_norm_g[l]),
        "vng": row(gm_vnorm_g[l]),
        "ws": gm_ws[l],
        "bsx": jnp.repeat(gm_bs[l].T, GM_HEAD_DIM, axis=1),
        "gog": row(gm_out_g[l]),
        "wo1": w_out[l][:D_SSD].astype(BF16),
        "wo2": w_out[l][D_SSD:].astype(BF16),
        "n2g": row(norm2_g[l]),
        "wg": ff_up[l][:, :D_FF].astype(BF16),
        "wval": ff_up[l][:, D_FF:].astype(BF16),
        "fcw": ff_conv_w[l],
        "fcb": row(ff_conv_b[l]),
        "wd": ff_down[l].astype(BF16),
    }


def kernel(x, c, ada_w, ada_b, norm1_g, norm2_g, w_in, ssd_conv_w, ssd_conv_b, ssd_dt_bias, ssd_a_log, ssd_d, ssd_norm_g, gm_vnorm_g, gm_ws, gm_bs, gm_out_g, w_out, ff_up, ff_conv_w, ff_conv_b, ff_down, final_g):
    depth = ada_w.shape[0]
    bsz = x.shape[0]
    mod = _ada(c, ada_w, ada_b).reshape(depth, bsz, N_MOD, D_MODEL)
    fg = final_g.reshape(1, D_MODEL)
    for l in range(depth):
        p = _layer_params(l, norm1_g, norm2_g, w_in, ssd_conv_w, ssd_conv_b,
                          ssd_dt_bias, ssd_a_log, ssd_d, ssd_norm_g,
                          gm_vnorm_g, gm_ws, gm_bs, gm_out_g, w_out, ff_up,
                          ff_conv_w, ff_conv_b, ff_down)
        x = _mixer(x, mod[l], p)
        x = _ffn(x, mod[l], p, fg, final=(l == depth - 1))
    return x
```

```python
import functools

import jax
import jax.numpy as jnp
from jax import lax
from jax.experimental import pallas as pl
from jax.experimental.pallas import tpu as pltpu

F32 = jnp.float32
BF16 = jnp.bfloat16

D_MODEL = 1024
SSD_HEAD_DIM = 64
SSD_HEADS = 16
D_SSD = 1024
SSD_GROUPS = 2
HEADS_PER_GROUP = 8
D_STATE = 128
SSD_CONV = 4
CHUNK = 128
CONV_DIM = D_SSD + 2 * SSD_GROUPS * D_STATE
GM_HEAD_DIM = 128
GM_HEADS = 8
D_GM = 1024
D_FF = 2816
FF_CONV = 3
N_MOD = 6
EPS = 1e-6
LOG2E = 1.4426950408889634

LANES = 128
SUBLANES = 8
HIST = SUBLANES
TOK = 256
N_CHUNKS = TOK // CHUNK
VMEM_LIMIT = 56 * 1024 * 1024
GROUP_W = HEADS_PER_GROUP * SSD_HEAD_DIM


def _dot(a, b):
    return jnp.dot(a, b, preferred_element_type=F32)


def _dot_nt(a, b):
    return lax.dot_general(a, b, (((1,), (1,)), ((), ())),
                           preferred_element_type=F32)


def _silu(x):
    half = 0.5 * x
    return half + half * jnp.tanh(half)


def _gelu(x):
    return 0.5 * x * (1.0 + lax.erf(x * 0.7071067811865476))


def _rms_scale(x):
    return lax.rsqrt(jnp.mean(x * x, axis=-1, keepdims=True) + EPS)


def _split3(x):
    hi = x.astype(BF16)
    r1 = x - hi.astype(F32)
    mid = r1.astype(BF16)
    lo = (r1 - mid.astype(F32)).astype(BF16)
    return hi, mid, lo


def _causal_conv(scr, tok, w_ref, b_ref, width):
    full = scr[0:HIST + tok, :]
    acc = b_ref[...] + w_ref[width - 1:width, :] * full[HIST:]
    for j in range(1, width):
        acc = acc + w_ref[width - 1 - j:width - j, :] * pltpu.roll(full, j, 0)[HIST:]
    return acc


def _slot_shapes(fields):
    return [pltpu.VMEM(shape, dt) for _, shape, dt in fields]


def _slot_dicts(refs, fields):
    n = len(fields)
    return [dict(zip([f[0] for f in fields], refs[i * n:(i + 1) * n]))
            for i in range(2)]


def _zero_slot(slot):
    for ref in slot.values():
        ref[...] = jnp.zeros(ref.shape, ref.dtype)


def _const_spec(shape):
    zeros = (0,) * len(shape)
    return pl.BlockSpec(shape, lambda *_: zeros, pipeline_mode=pl.Buffered(1))


def _ada_kernel(c_ref, w_ref, b_ref, o_ref):
    ca = _silu(c_ref[...]).astype(BF16)
    o_ref[0] = _dot(ca, w_ref[0].astype(BF16)) + b_ref[0]


def _ada(c, ada_w, ada_b):
    n_layers, _, n_out = ada_w.shape
    bsz = c.shape[0]
    tn = 1536
    return pl.pallas_call(
        _ada_kernel,
        grid=(n_layers, n_out // tn),
        in_specs=[
            pl.BlockSpec((bsz, D_MODEL), lambda l, j: (0, 0)),
            pl.BlockSpec((1, D_MODEL, tn), lambda l, j: (l, 0, j)),
            pl.BlockSpec((1, 1, tn), lambda l, j: (l, 0, j)),
        ],
        out_specs=pl.BlockSpec((1, bsz, tn), lambda l, j: (l, 0, j)),
        out_shape=jax.ShapeDtypeStruct((n_layers, bsz, n_out), F32),
        compiler_params=pltpu.CompilerParams(
            dimension_semantics=("arbitrary", "arbitrary"),
            vmem_limit_bytes=VMEM_LIMIT),
        name="ada_mod",
    )(c, ada_w, ada_b.reshape(n_layers, 1, n_out))


_PROJ_FIELDS = (
    ("x", (TOK, D_MODEL), F32),
    ("z", (TOK, D_SSD), F32),
    ("xs", (TOK, D_SSD), F32),
    ("bc", (TOK, 2 * SSD_GROUPS * D_STATE), F32),
    ("dt", (SSD_HEADS, TOK), F32),
    ("u", (TOK, D_GM), F32),
    ("vn", (TOK, D_GM), BF16),
)
_DECAY_FIELDS = (
    ("x", (TOK, D_MODEL), F32),
    ("sz", (TOK, D_SSD), F32),
    ("dsx", (TOK, D_SSD), F32),
    ("xsb", (TOK, D_SSD), BF16),
    ("gl", (TOK, D_GM), F32),
    ("lhs", (N_CHUNKS * SSD_HEADS, CHUNK, CHUNK + D_STATE), BF16),
    ("xw", (TOK, D_SSD), BF16),
    ("bt", (N_CHUNKS * SSD_GROUPS, D_STATE, CHUNK), BF16),
    ("dec", (N_CHUNKS * SSD_GROUPS, SUBLANES, GROUP_W), F32),
)


def _project_stage(x_ref, mod_ref, w, xbc_scr, slot):
    tok = x_ref.shape[1]
    x = x_ref[0]
    mod = mod_ref[0]
    sh1, sc1 = mod[0:1], mod[1:2]
    h = (x * _rms_scale(x)) * (w["n1g"][...] * (1.0 + sc1)) + sh1
    hb = h.astype(BF16)
    slot["x"][...] = x
    proj = _dot(hb, w["win"][...])
    slot["z"][...] = proj[:, :D_SSD]
    xbc_scr[HIST:HIST + tok, :] = proj[:, D_SSD:D_SSD + CONV_DIM]
    u_raw = proj[:, D_SSD + CONV_DIM:D_SSD + CONV_DIM + D_GM]
    v_raw = proj[:, D_SSD + CONV_DIM + D_GM:]
    slot["dt"][...] = _dot_nt(w["wdtt"][...], hb)
    acc = _causal_conv(xbc_scr, tok, w["cw"], w["cb"], SSD_CONV)
    xbc_scr[0:HIST, :] = xbc_scr[tok:tok + HIST, :]
    xbc = _silu(acc)
    slot["xs"][...] = xbc[:, :D_SSD]
    slot["bc"][...] = xbc[:, D_SSD:]
    slot["u"][...] = _gelu(u_raw)
    v = _gelu(v_raw)
    slot["vn"][...] = ((v * _rms_scale(v)) * w["vng"][...]).astype(BF16)


def _decay_stage(src, w, slot):
    tok = src["xs"].shape[0]
    n_chunks = tok // CHUNK
    xs = src["xs"][...]
    bc = src["bc"][...]
    bmat = bc[:, :SSD_GROUPS * D_STATE]
    cmat = bc[:, SSD_GROUPS * D_STATE:]
    slot["x"][...] = src["x"][...]
    slot["sz"][...] = _silu(src["z"][...])
    slot["xsb"][...] = xs.astype(BF16)
    slot["dsx"][...] = w["dsk"][...] * xs

    dt_in = src["dt"][...] + w["dtb"][...]
    dt_t = jnp.maximum(dt_in, 0.0) + jnp.log1p(jnp.exp(-jnp.abs(dt_in)))
    dta_t = dt_t * (-jnp.exp(w["alog"][...]))
    log2_dt_t = jnp.log(dt_t) * LOG2E

    li = lax.broadcasted_iota(jnp.int32, (CHUNK, CHUNK), 0)
    si = lax.broadcasted_iota(jnp.int32, (CHUNK, CHUNK), 1)
    causal = li >= si
    triu_b = jnp.where(li <= si, 1.0, 0.0).astype(BF16)
    low_half = si[0:1] < SSD_HEAD_DIM

    for c in range(tok // CHUNK):
        r0 = c * CHUNK
        hi, mid, lo = _split3(dta_t[:, r0:r0 + CHUNK])
        a2_t = (_dot(hi, triu_b) + _dot(mid, triu_b) + _dot(lo, triu_b)) * LOG2E
        dt_c = dt_t[:, r0:r0 + CHUNK]
        row_t = a2_t - log2_dt_t[:, r0:r0 + CHUNK]
        w_t = dt_c * jnp.exp2(a2_t[:, CHUNK - 1:CHUNK] - a2_t)
        for g in range(SSD_GROUPS):
            b_blk = bmat[r0:r0 + CHUNK, g * D_STATE:(g + 1) * D_STATE]
            c_blk = cmat[r0:r0 + CHUNK, g * D_STATE:(g + 1) * D_STATE]
            cb = _dot_nt(c_blk.astype(BF16), b_blk.astype(BF16))
            slot["bt"][c * SSD_GROUPS + g] = b_blk.T.astype(BF16)
            w_parts, dec_parts = [], []
            for j in range(HEADS_PER_GROUP // 2):
                w_pair, e_pair = [], []
                for hd in (g * HEADS_PER_GROUP + 2 * j,
                           g * HEADS_PER_GROUP + 2 * j + 1):
                    col = jnp.broadcast_to(a2_t[hd:hd + 1, :], (CHUNK, CHUNK)).T
                    seg = jnp.where(causal, col - row_t[hd:hd + 1, :], -jnp.inf)
                    att = cb * jnp.exp2(seg)
                    e_col = jnp.exp2(col)
                    cd = c_blk * e_col
                    slot["lhs"][c * SSD_HEADS + hd] = jnp.concatenate(
                        [att.astype(BF16), cd.astype(BF16)], axis=1)
                    w_pair.append(jnp.broadcast_to(
                        w_t[hd:hd + 1, :], (SSD_HEAD_DIM, CHUNK)))
                    e_pair.append(e_col[CHUNK - 1:CHUNK, :])
                w_parts.append(jnp.concatenate(w_pair, axis=0).T)
                dec_parts.append(jnp.where(low_half, e_pair[0], e_pair[1]))
            w_exp = jnp.concatenate(w_parts, axis=1)
            slot["xw"][r0:r0 + CHUNK, g * GROUP_W:(g + 1) * GROUP_W] = (
                xs[r0:r0 + CHUNK, g * GROUP_W:(g + 1) * GROUP_W] * w_exp
            ).astype(BF16)
            slot["dec"][c * SSD_GROUPS + g, 0:1, :] = jnp.concatenate(
                dec_parts, axis=1)

    vn = src["vn"][...]
    sv_heads = []
    for hd in range(GM_HEADS):
        ws_m = jnp.where(causal, w["ws"][hd], 0.0).astype(BF16)
        v_h = jnp.concatenate(
            [vn[c * CHUNK:(c + 1) * CHUNK,
                hd * GM_HEAD_DIM:(hd + 1) * GM_HEAD_DIM]
             for c in range(n_chunks)], axis=1)
        sv_heads.append(_dot(ws_m, v_h))
    sv = jnp.concatenate(
        [jnp.concatenate([s[:, c * CHUNK:(c + 1) * CHUNK] for s in sv_heads],
                         axis=1) + w["bsx"][...]
         for c in range(n_chunks)], axis=0)
    slot["gl"][...] = src["u"][...] * sv


def _scan_out_stage(slot, mod_ref, w, state_scr, o_ref):
    tok = slot["sz"].shape[0]
    n_chunks = tok // CHUNK
    low_half = lax.broadcasted_iota(jnp.int32, (CHUNK, CHUNK), 1) < SSD_HEAD_DIM

    y_rows = []
    for c in range(n_chunks):
        r0 = c * CHUNK
        y_parts = []
        for g in range(SSD_GROUPS):
            state = state_scr[g]
            rhs_g = jnp.concatenate(
                [slot["xsb"][r0:r0 + CHUNK, g * GROUP_W:(g + 1) * GROUP_W],
                 state.astype(BF16)], axis=0)
            for j in range(HEADS_PER_GROUP // 2):
                rhs = rhs_g[:, j * LANES:(j + 1) * LANES]
                hd = c * SSD_HEADS + g * HEADS_PER_GROUP + 2 * j
                y_even = _dot(slot["lhs"][hd], rhs)
                y_odd = _dot(slot["lhs"][hd + 1], rhs)
                y_parts.append(jnp.where(low_half, y_even, y_odd))
            dec = slot["dec"][c * SSD_GROUPS + g, 0:1, :]
            state_scr[g] = state * dec + _dot(
                slot["bt"][c * SSD_GROUPS + g],
                slot["xw"][r0:r0 + CHUNK, g * GROUP_W:(g + 1) * GROUP_W])
        y_rows.append(jnp.concatenate(y_parts, axis=1))
    y_ssd = jnp.concatenate(y_rows, axis=0)

    yg_all = (y_ssd + slot["dsx"][...]) * slot["sz"][...]
    parts = []
    for g in range(SSD_GROUPS):
        yg = yg_all[:, g * GROUP_W:(g + 1) * GROUP_W]
        parts.append((yg * _rms_scale(yg))
                     * w["ssdg"][:, g * GROUP_W:(g + 1) * GROUP_W])
    yn = jnp.concatenate(parts, axis=1).astype(BF16)
    gl = slot["gl"][...]
    gn = ((gl * _rms_scale(gl)) * w["gog"][...]).astype(BF16)
    out = _dot(yn, w["wo1"][...]) + _dot(gn, w["wo2"][...])
    g1 = mod_ref[0][2:3]
    o_ref[0] = slot["x"][...] + g1 * out


_MIXER_CONSTS = ("n1g", "win", "wdtt", "cw", "cb", "dtb",
                 "alog", "dsk", "ssdg", "vng", "ws", "bsx", "gog", "wo1", "wo2")


def _mixer_kernel(*refs, tiles_per_seq):
    n_const = len(_MIXER_CONSTS)
    x_ref, mod_in_ref, mod_out_ref = refs[:3]
    w = dict(zip(_MIXER_CONSTS, refs[3:3 + n_const]))
    o_ref = refs[3 + n_const]
    xbc_scr, state_scr = refs[4 + n_const:6 + n_const]
    n_proj = 2 * len(_PROJ_FIELDS)
    proj_slots = _slot_dicts(refs[6 + n_const:6 + n_const + n_proj], _PROJ_FIELDS)
    decay_slots = _slot_dicts(refs[6 + n_const + n_proj:], _DECAY_FIELDS)
    k = pl.program_id(0)

    @pl.when(k == 0)
    def _():
        _zero_slot(proj_slots[1])
        _zero_slot(decay_slots[0])

    @pl.when(k % tiles_per_seq == 0)
    def _():
        xbc_scr[0:HIST, :] = jnp.zeros((HIST, CONV_DIM), F32)

    @pl.when(jnp.logical_or(k <= 1, (k - 2) % tiles_per_seq == 0))
    def _():
        state_scr[...] = jnp.zeros(state_scr.shape, F32)

    def step(p):
        _scan_out_stage(decay_slots[p], mod_out_ref, w, state_scr, o_ref)
        _decay_stage(proj_slots[1 - p], w, decay_slots[1 - p])
        _project_stage(x_ref, mod_in_ref, w, xbc_scr, proj_slots[p])

    @pl.when(k % 2 == 0)
    def _():
        step(0)

    @pl.when(k % 2 == 1)
    def _():
        step(1)


def _mixer(x, mod, p):
    bsz, seq, _ = x.shape
    tps = seq // TOK
    n_tiles = bsz * tps
    consts = [p[name] for name in _MIXER_CONSTS]

    def in_tile(k):
        t = jnp.minimum(k, n_tiles - 1)
        return t // tps, t % tps

    def out_tile(k):
        t = jnp.maximum(k - 2, 0)
        return t // tps, t % tps

    return pl.pallas_call(
        functools.partial(_mixer_kernel, tiles_per_seq=tps),
        grid=(n_tiles + 2,),
        in_specs=[
            pl.BlockSpec((1, TOK, D_MODEL), lambda k: (*in_tile(k), 0)),
            pl.BlockSpec((1, N_MOD, D_MODEL), lambda k: (in_tile(k)[0], 0, 0)),
            pl.BlockSpec((1, N_MOD, D_MODEL), lambda k: (out_tile(k)[0], 0, 0)),
        ] + [_const_spec(a.shape) for a in consts],
        out_specs=pl.BlockSpec((1, TOK, D_MODEL), lambda k: (*out_tile(k), 0)),
        out_shape=jax.ShapeDtypeStruct(x.shape, F32),
        scratch_shapes=[
            pltpu.VMEM((HIST + TOK, CONV_DIM), F32),
            pltpu.VMEM((SSD_GROUPS, D_STATE, GROUP_W), F32),
        ] + 2 * _slot_shapes(_PROJ_FIELDS) + 2 * _slot_shapes(_DECAY_FIELDS),
        compiler_params=pltpu.CompilerParams(
            dimension_semantics=("arbitrary",),
            vmem_limit_bytes=VMEM_LIMIT),
        name="mixer",
    )(x, mod, mod, *consts)


def _ffn_up_stage(x_ref, mod_ref, w, gate_scr, slot):
    tok = x_ref.shape[1]
    x = x_ref[0]
    mod = mod_ref[0]
    sh2, sc2 = mod[3:4], mod[4:5]
    h = (x * _rms_scale(x)) * (w["n2g"][...] * (1.0 + sc2)) + sh2
    hb = h.astype(BF16)
    slot["x"][...] = x
    gate_scr[HIST:HIST + tok, :] = _dot(hb, w["wg"][...])
    val = _dot(hb, w["wval"][...])
    acc = _causal_conv(gate_scr, tok, w["fcw"], w["fcb"], FF_CONV)
    gate_scr[0:HIST, :] = gate_scr[tok:tok + HIST, :]
    slot["act"][...] = (_silu(acc) * val).astype(BF16)


def _ffn_down_stage(slot, mod_ref, w, o_ref, final):
    g2 = mod_ref[0][5:6]
    xo = slot["x"][...] + g2 * _dot(slot["act"][...], w["wd"][...])
    if final:
        xo = (xo * _rms_scale(xo)) * w["fg"][...]
    o_ref[0] = xo


_FFN_CONSTS = ("n2g", "wg", "wval", "fcw", "fcb", "wd", "fg")
_FFN_FIELDS = (("x", (TOK, D_MODEL), F32), ("act", (TOK, D_FF), BF16))


def _ffn_kernel(*refs, tiles_per_seq, final):
    n_const = len(_FFN_CONSTS)
    x_ref, mod_in_ref, mod_out_ref = refs[:3]
    w = dict(zip(_FFN_CONSTS, refs[3:3 + n_const]))
    o_ref = refs[3 + n_const]
    gate_scr = refs[4 + n_const]
    slots = _slot_dicts(refs[5 + n_const:], _FFN_FIELDS)
    k = pl.program_id(0)

    @pl.when(k == 0)
    def _():
        _zero_slot(slots[1])

    @pl.when(k % tiles_per_seq == 0)
    def _():
        gate_scr[0:HIST, :] = jnp.zeros((HIST, D_FF), F32)

    def step(p):
        _ffn_down_stage(slots[1 - p], mod_out_ref, w, o_ref, final)
        _ffn_up_stage(x_ref, mod_in_ref, w, gate_scr, slots[p])

    @pl.when(k % 2 == 0)
    def _():
        step(0)

    @pl.when(k % 2 == 1)
    def _():
        step(1)


def _ffn(x, mod, p, final_g, final):
    bsz, seq, _ = x.shape
    tps = seq // TOK
    n_tiles = bsz * tps
    consts = [p[name] for name in _FFN_CONSTS[:-1]] + [final_g]

    def in_tile(k):
        t = jnp.minimum(k, n_tiles - 1)
        return t // tps, t % tps

    def out_tile(k):
        t = jnp.maximum(k - 1, 0)
        return t // tps, t % tps

    return pl.pallas_call(
        functools.partial(_ffn_kernel, tiles_per_seq=tps, final=final),
        grid=(n_tiles + 1,),
        in_specs=[
            pl.BlockSpec((1, TOK, D_MODEL), lambda k: (*in_tile(k), 0)),
            pl.BlockSpec((1, N_MOD, D_MODEL), lambda k: (in_tile(k)[0], 0, 0)),
            pl.BlockSpec((1, N_MOD, D_MODEL), lambda k: (out_tile(k)[0], 0, 0)),
        ] + [_const_spec(a.shape) for a in consts],
        out_specs=pl.BlockSpec((1, TOK, D_MODEL), lambda k: (*out_tile(k), 0)),
        out_shape=jax.ShapeDtypeStruct(x.shape, F32),
        scratch_shapes=[pltpu.VMEM((HIST + TOK, D_FF), F32)]
        + 2 * _slot_shapes(_FFN_FIELDS),
        compiler_params=pltpu.CompilerParams(
            dimension_semantics=("arbitrary",),
            vmem_limit_bytes=VMEM_LIMIT),
        name="ffn_final" if final else "ffn",
    )(x, mod, mod, *consts)


def _layer_params(l, norm1_g, norm2_g, w_in, ssd_conv_w, ssd_conv_b,
                  ssd_dt_bias, ssd_a_log, ssd_d, ssd_norm_g, gm_vnorm_g, gm_ws,
                  gm_bs, gm_out_g, w_out, ff_up, ff_conv_w, ff_conv_b, ff_down):
    o_xbc = D_SSD
    o_dt = o_xbc + CONV_DIM
    o_u = o_dt + SSD_HEADS
    w = w_in[l]
    row = lambda a: a.reshape(1, -1)
    col = lambda a: a.reshape(-1, 1)
    return {
        "n1g": row(norm1_g[l]),
        "win": jnp.concatenate([w[:, :o_dt], w[:, o_u:]], axis=1).astype(BF16),
        "wdtt": w[:, o_dt:o_u].T.astype(BF16),
        "cw": ssd_conv_w[l],
        "cb": row(ssd_conv_b[l]),
        "dtb": col(ssd_dt_bias[l]),
        "alog": col(ssd_a_log[l]),
        "dsk": row(jnp.repeat(ssd_d[l], SSD_HEAD_DIM)),
        "ssdg": row(ssd_norm_g[l]),
        "vng": row(gm_vnorm_g[l]),
        "ws": gm_ws[l],
        "bsx": jnp.repeat(gm_bs[l].T, GM_HEAD_DIM, axis=1),
        "gog": row(gm_out_g[l]),
        "wo1": w_out[l][:D_SSD].astype(BF16),
        "wo2": w_out[l][D_SSD:].astype(BF16),
        "n2g": row(norm2_g[l]),
        "wg": ff_up[l][:, :D_FF].astype(BF16),
        "wval": ff_up[l][:, D_FF:].astype(BF16),
        "fcw": ff_conv_w[l],
        "fcb": row(ff_conv_b[l]),
        "wd": ff_down[l].astype(BF16),
    }


def kernel(x, c, ada_w, ada_b, norm1_g, norm2_g, w_in, ssd_conv_w, ssd_conv_b, ssd_dt_bias, ssd_a_log, ssd_d, ssd_norm_g, gm_vnorm_g, gm_ws, gm_bs, gm_out_g, w_out, ff_up, ff_conv_w, ff_conv_b, ff_down, final_g):
    depth = ada_w.shape[0]
    bsz = x.shape[0]
    mod = _ada(c, ada_w, ada_b).reshape(depth, bsz, N_MOD, D_MODEL)
    fg = final_g.reshape(1, D_MODEL)
    for l in range(depth):
        p = _layer_params(l, norm1_g, norm2_g, w_in, ssd_conv_w, ssd_conv_b,
                          ssd_dt_bias, ssd_a_log, ssd_d, ssd_norm_g,
                          gm_vnorm_g, gm_ws, gm_bs, gm_out_g, w_out, ff_up,
                          ff_conv_w, ff_conv_b, ff_down)
        x = _mixer(x, mod[l], p)
        x = _ffn(x, mod[l], p, fg, final=(l == depth - 1))
    return x
```

```python
import functools

import jax
import jax.numpy as jnp
from jax import lax
from jax.experimental import pallas as pl
from jax.experimental.pallas import tpu as pltpu

F32 = jnp.float32
BF16 = jnp.bfloat16

D_MODEL = 1024
SSD_HEAD_DIM = 64
SSD_HEADS = 16
D_SSD = 1024
SSD_GROUPS = 2
HEADS_PER_GROUP = 8
D_STATE = 128
SSD_CONV = 4
CHUNK = 128
CONV_DIM = D_SSD + 2 * SSD_GROUPS * D_STATE
GM_HEAD_DIM = 128
GM_HEADS = 8
D_GM = 1024
D_FF = 2816
FF_CONV = 3
N_MOD = 6
EPS = 1e-6
LOG2E = 1.4426950408889634

LANES = 128
SUBLANES = 8
HIST = SUBLANES
TOK = 256
N_CHUNKS = TOK // CHUNK
VMEM_LIMIT = 56 * 1024 * 1024
GROUP_W = HEADS_PER_GROUP * SSD_HEAD_DIM


def _dot(a, b):
    return jnp.dot(a, b, preferred_element_type=F32)


def _dot_nt(a, b):
    return lax.dot_general(a, b, (((1,), (1,)), ((), ())),
                           preferred_element_type=F32)


def _silu(x):
    half = 0.5 * x
    return half + half * jnp.tanh(half)


def _gelu(x):
    return 0.5 * x * (1.0 + lax.erf(x * 0.7071067811865476))


def _rms_scale(x):
    return lax.rsqrt(jnp.mean(x * x, axis=-1, keepdims=True) + EPS)


def _split3(x):
    hi = x.astype(BF16)
    r1 = x - hi.astype(F32)
    mid = r1.astype(BF16)
    lo = (r1 - mid.astype(F32)).astype(BF16)
    return hi, mid, lo


def _causal_conv(scr, tok, w_ref, b_ref, width):
    full = scr[0:HIST + tok, :]
    acc = b_ref[...] + w_ref[width - 1:width, :] * full[HIST:]
    for j in range(1, width):
        acc = acc + w_ref[width - 1 - j:width - j, :] * pltpu.roll(full, j, 0)[HIST:]
    return acc


def _slot_shapes(fields):
    return [pltpu.VMEM(shape, dt) for _, shape, dt in fields]


def _slot_dicts(refs, fields):
    n = len(fields)
    return [dict(zip([f[0] for f in fields], refs[i * n:(i + 1) * n]))
            for i in range(2)]


def _zero_slot(slot):
    for ref in slot.values():
        ref[...] = jnp.zeros(ref.shape, ref.dtype)


def _layer_spec(stacked, l, block=None, index=None):
    block = (1,) + (stacked.shape[1:] if block is None else block)
    index = (l,) + ((0,) * (len(block) - 1) if index is None else index)
    return pl.BlockSpec(block, lambda *_: index, pipeline_mode=pl.Buffered(1))


def _ada_kernel(c_ref, w_ref, b_ref, o_ref):
    ca = _silu(c_ref[...]).astype(BF16)
    o_ref[0] = _dot(ca, w_ref[0].astype(BF16)) + b_ref[0]


def _ada(c, ada_w, ada_b):
    n_layers, _, n_out = ada_w.shape
    bsz = c.shape[0]
    tn = 1536
    return pl.pallas_call(
        _ada_kernel,
        grid=(n_layers, n_out // tn),
        in_specs=[
            pl.BlockSpec((bsz, D_MODEL), lambda l, j: (0, 0)),
            pl.BlockSpec((1, D_MODEL, tn), lambda l, j: (l, 0, j)),
            pl.BlockSpec((1, 1, tn), lambda l, j: (l, 0, j)),
        ],
        out_specs=pl.BlockSpec((1, bsz, tn), lambda l, j: (l, 0, j)),
        out_shape=jax.ShapeDtypeStruct((n_layers, bsz, n_out), F32),
        compiler_params=pltpu.CompilerParams(
            dimension_semantics=("arbitrary", "arbitrary"),
            vmem_limit_bytes=VMEM_LIMIT),
        name="ada_mod",
    )(c, ada_w, ada_b.reshape(n_layers, 1, n_out))


_PROJ_FIELDS = (
    ("x", (TOK, D_MODEL), F32),
    ("z", (TOK, D_SSD), F32),
    ("xs", (TOK, D_SSD), F32),
    ("bc", (TOK, 2 * SSD_GROUPS * D_STATE), F32),
    ("dt", (SSD_HEADS, TOK), F32),
    ("u", (TOK, D_GM), F32),
    ("vn", (TOK, D_GM), BF16),
)
_DECAY_FIELDS = (
    ("x", (TOK, D_MODEL), F32),
    ("sz", (TOK, D_SSD), F32),
    ("dsx", (TOK, D_SSD), F32),
    ("xsb", (TOK, D_SSD), BF16),
    ("gl", (TOK, D_GM), F32),
    ("lhs", (N_CHUNKS * SSD_HEADS, CHUNK, CHUNK + D_STATE), BF16),
    ("xw", (TOK, D_SSD), BF16),
    ("bt", (N_CHUNKS * SSD_GROUPS, D_STATE, CHUNK), BF16),
    ("dec", (N_CHUNKS * SSD_GROUPS, SUBLANES, GROUP_W), F32),
)


def _project_stage(x_ref, mod_ref, w, xbc_scr, hb_scr, slot):
    tok = x_ref.shape[1]
    x = x_ref[0]
    mod = mod_ref[0, 0]
    sh1, sc1 = mod[0:1], mod[1:2]
    h = (x * _rms_scale(x)) * (w["n1g"][...] * (1.0 + sc1)) + sh1
    hb_scr[...] = h.astype(BF16)
    slot["x"][...] = x
    proj = _dot(hb_scr[...], w["win"][...])
    slot["z"][...] = proj[:, :D_SSD]
    xbc_scr[HIST:HIST + tok, :] = proj[:, D_SSD:D_SSD + CONV_DIM]
    u_raw = proj[:, D_SSD + CONV_DIM:D_SSD + CONV_DIM + D_GM]
    v_raw = proj[:, D_SSD + CONV_DIM + D_GM:]
    slot["dt"][...] = _dot_nt(w["wdtt"][...], hb_scr[...])
    acc = _causal_conv(xbc_scr, tok, w["cw"], w["cb"], SSD_CONV)
    xbc_scr[0:HIST, :] = xbc_scr[tok:tok + HIST, :]
    xbc = _silu(acc)
    slot["xs"][...] = xbc[:, :D_SSD]
    slot["bc"][...] = xbc[:, D_SSD:]
    slot["u"][...] = _gelu(u_raw)
    v = _gelu(v_raw)
    slot["vn"][...] = ((v * _rms_scale(v)) * w["vng"][...]).astype(BF16)


def _decay_stage(src, w, slot):
    tok = src["xs"].shape[0]
    n_chunks = tok // CHUNK
    xs = src["xs"][...]
    bc = src["bc"][...]
    bmat = bc[:, :SSD_GROUPS * D_STATE]
    cmat = bc[:, SSD_GROUPS * D_STATE:]
    slot["x"][...] = src["x"][...]
    slot["sz"][...] = _silu(src["z"][...])
    slot["xsb"][...] = xs.astype(BF16)
    slot["dsx"][...] = w["dsk"][...] * xs

    dt_in = src["dt"][...] + w["dtb"][...]
    dt_t = jnp.maximum(dt_in, 0.0) + jnp.log1p(jnp.exp(-jnp.abs(dt_in)))
    dta_t = dt_t * (-jnp.exp(w["alog"][...]))
    log2_dt_t = jnp.log(dt_t) * LOG2E

    li = lax.broadcasted_iota(jnp.int32, (CHUNK, CHUNK), 0)
    si = lax.broadcasted_iota(jnp.int32, (CHUNK, CHUNK), 1)
    causal = li >= si
    triu_b = jnp.where(li <= si, 1.0, 0.0).astype(BF16)
    low_half = si[0:1] < SSD_HEAD_DIM

    for c in range(tok // CHUNK):
        r0 = c * CHUNK
        hi, mid, lo = _split3(dta_t[:, r0:r0 + CHUNK])
        a2_t = (_dot(hi, triu_b) + _dot(mid, triu_b) + _dot(lo, triu_b)) * LOG2E
        dt_c = dt_t[:, r0:r0 + CHUNK]
        row_t = a2_t - log2_dt_t[:, r0:r0 + CHUNK]
        w_t = dt_c * jnp.exp2(a2_t[:, CHUNK - 1:CHUNK] - a2_t)
        for g in range(SSD_GROUPS):
            b_blk = bmat[r0:r0 + CHUNK, g * D_STATE:(g + 1) * D_STATE]
            c_blk = cmat[r0:r0 + CHUNK, g * D_STATE:(g + 1) * D_STATE]
            cb = _dot_nt(c_blk.astype(BF16), b_blk.astype(BF16))
            slot["bt"][c * SSD_GROUPS + g] = b_blk.T.astype(BF16)
            w_parts, dec_parts = [], []
            for j in range(HEADS_PER_GROUP // 2):
                w_pair, e_pair = [], []
                for hd in (g * HEADS_PER_GROUP + 2 * j,
                           g * HEADS_PER_GROUP + 2 * j + 1):
                    col = jnp.broadcast_to(a2_t[hd:hd + 1, :], (CHUNK, CHUNK)).T
                    seg = jnp.where(causal, col - row_t[hd:hd + 1, :], -jnp.inf)
                    att = cb * jnp.exp2(seg)
                    e_col = jnp.exp2(col)
                    cd = c_blk * e_col
                    slot["lhs"][c * SSD_HEADS + hd] = jnp.concatenate(
                        [att.astype(BF16), cd.astype(BF16)], axis=1)
                    w_pair.append(jnp.broadcast_to(
                        w_t[hd:hd + 1, :], (SSD_HEAD_DIM, CHUNK)))
                    e_pair.append(e_col[CHUNK - 1:CHUNK, :])
                w_parts.append(jnp.concatenate(w_pair, axis=0).T)
                dec_parts.append(jnp.where(low_half, e_pair[0], e_pair[1]))
            w_exp = jnp.concatenate(w_parts, axis=1)
            slot["xw"][r0:r0 + CHUNK, g * GROUP_W:(g + 1) * GROUP_W] = (
                xs[r0:r0 + CHUNK, g * GROUP_W:(g + 1) * GROUP_W] * w_exp
            ).astype(BF16)
            slot["dec"][c * SSD_GROUPS + g, 0:1, :] = jnp.concatenate(
                dec_parts, axis=1)

    vn = src["vn"][...]
    sv_heads = []
    for hd in range(GM_HEADS):
        ws_m = jnp.where(causal, w["ws"][hd], 0.0).astype(BF16)
        v_h = jnp.concatenate(
            [vn[c * CHUNK:(c + 1) * CHUNK,
                hd * GM_HEAD_DIM:(hd + 1) * GM_HEAD_DIM]
             for c in range(n_chunks)], axis=1)
        sv_heads.append(_dot(ws_m, v_h))
    sv = jnp.concatenate(
        [jnp.concatenate([s[:, c * CHUNK:(c + 1) * CHUNK] for s in sv_heads],
                         axis=1) + w["bsx"][...]
         for c in range(n_chunks)], axis=0)
    slot["gl"][...] = src["u"][...] * sv


def _scan_out_stage(slot, mod_ref, w, state_scr, mix_scr, o_ref):
    tok = slot["sz"].shape[0]
    n_chunks = tok // CHUNK
    low_half = lax.broadcasted_iota(jnp.int32, (CHUNK, CHUNK), 1) < SSD_HEAD_DIM

    y_rows = []
    for c in range(n_chunks):
        r0 = c * CHUNK
        y_parts = []
        for g in range(SSD_GROUPS):
            state = state_scr[g]
            rhs_g = jnp.concatenate(
                [slot["xsb"][r0:r0 + CHUNK, g * GROUP_W:(g + 1) * GROUP_W],
                 state.astype(BF16)], axis=0)
            for j in range(HEADS_PER_GROUP // 2):
                rhs = rhs_g[:, j * LANES:(j + 1) * LANES]
                hd = c * SSD_HEADS + g * HEADS_PER_GROUP + 2 * j
                y_even = _dot(slot["lhs"][hd], rhs)
                y_odd = _dot(slot["lhs"][hd + 1], rhs)
                y_parts.append(jnp.where(low_half, y_even, y_odd))
            dec = slot["dec"][c * SSD_GROUPS + g, 0:1, :]
            state_scr[g] = state * dec + _dot(
                slot["bt"][c * SSD_GROUPS + g],
                slot["xw"][r0:r0 + CHUNK, g * GROUP_W:(g + 1) * GROUP_W])
        y_rows.append(jnp.concatenate(y_parts, axis=1))
    y_ssd = jnp.concatenate(y_rows, axis=0)

    yg_all = (y_ssd + slot["dsx"][...]) * slot["sz"][...]
    for g in range(SSD_GROUPS):
        yg = yg_all[:, g * GROUP_W:(g + 1) * GROUP_W]
        mix_scr[:, g * GROUP_W:(g + 1) * GROUP_W] = (
            (yg * _rms_scale(yg)) * w["ssdg"][:, g * GROUP_W:(g + 1) * GROUP_W]
        ).astype(BF16)
    gl = slot["gl"][...]
    mix_scr[:, D_SSD:] = ((gl * _rms_scale(gl)) * w["gog"][...]).astype(BF16)
    out = _dot(mix_scr[...], w["wo"][...])
    g1 = mod_ref[0, 0][2:3]
    o_ref[0] = slot["x"][...] + g1 * out


_MIXER_CONSTS = ("n1g", "win", "wdtt", "cw", "cb", "dtb",
                 "alog", "dsk", "ssdg", "vng", "ws", "bsx", "gog", "wo")


def _mixer_kernel(*refs, tiles_per_seq):
    n_const = len(_MIXER_CONSTS)
    x_ref, mod_in_ref, mod_out_ref = refs[:3]
    w = {name: ref.at[0] for name, ref in zip(_MIXER_CONSTS, refs[3:3 + n_const])}
    o_ref = refs[3 + n_const]
    xbc_scr, state_scr, hb_scr, mix_scr = refs[4 + n_const:8 + n_const]
    n_proj = 2 * len(_PROJ_FIELDS)
    proj_slots = _slot_dicts(refs[8 + n_const:8 + n_const + n_proj], _PROJ_FIELDS)
    decay_slots = _slot_dicts(refs[8 + n_const + n_proj:], _DECAY_FIELDS)
    k = pl.program_id(0)

    @pl.when(k == 0)
    def _():
        _zero_slot(proj_slots[1])
        _zero_slot(decay_slots[0])

    @pl.when(k % tiles_per_seq == 0)
    def _():
        xbc_scr[0:HIST, :] = jnp.zeros((HIST, CONV_DIM), F32)

    @pl.when(jnp.logical_or(k <= 1, (k - 2) % tiles_per_seq == 0))
    def _():
        state_scr[...] = jnp.zeros(state_scr.shape, F32)

    def step(p):
        _scan_out_stage(decay_slots[p], mod_out_ref, w, state_scr, mix_scr, o_ref)
        _decay_stage(proj_slots[1 - p], w, decay_slots[1 - p])
        _project_stage(x_ref, mod_in_ref, w, xbc_scr, hb_scr, proj_slots[p])

    @pl.when(k % 2 == 0)
    def _():
        step(0)

    @pl.when(k % 2 == 1)
    def _():
        step(1)


def _mixer(x, mod, p, l):
    bsz, seq, _ = x.shape
    tps = seq // TOK
    n_tiles = bsz * tps
    consts = [p[name] for name in _MIXER_CONSTS]

    def in_tile(k):
        t = jnp.minimum(k, n_tiles - 1)
        return t // tps, t % tps

    def out_tile(k):
        t = jnp.maximum(k - 2, 0)
        return t // tps, t % tps

    return pl.pallas_call(
        functools.partial(_mixer_kernel, tiles_per_seq=tps),
        grid=(n_tiles + 2,),
        in_specs=[
            pl.BlockSpec((1, TOK, D_MODEL), lambda k: (*in_tile(k), 0)),
            pl.BlockSpec((1, 1, N_MOD, D_MODEL),
                         lambda k: (l, in_tile(k)[0], 0, 0)),
            pl.BlockSpec((1, 1, N_MOD, D_MODEL),
                         lambda k: (l, out_tile(k)[0], 0, 0)),
        ] + [_layer_spec(a, l) for a in consts],
        out_specs=pl.BlockSpec((1, TOK, D_MODEL), lambda k: (*out_tile(k), 0)),
        out_shape=jax.ShapeDtypeStruct(x.shape, F32),
        scratch_shapes=[
            pltpu.VMEM((HIST + TOK, CONV_DIM), F32),
            pltpu.VMEM((SSD_GROUPS, D_STATE, GROUP_W), F32),
            pltpu.VMEM((TOK, D_MODEL), BF16),
            pltpu.VMEM((TOK, D_SSD + D_GM), BF16),
        ] + 2 * _slot_shapes(_PROJ_FIELDS) + 2 * _slot_shapes(_DECAY_FIELDS),
        compiler_params=pltpu.CompilerParams(
            dimension_semantics=("arbitrary",),
            vmem_limit_bytes=VMEM_LIMIT),
        name="mixer",
    )(x, mod, mod, *consts)


def _ffn_up_stage(x_ref, mod_ref, w, gate_scr, hb_scr, slot):
    tok = x_ref.shape[1]
    x = x_ref[0]
    mod = mod_ref[0, 0]
    sh2, sc2 = mod[3:4], mod[4:5]
    h = (x * _rms_scale(x)) * (w["n2g"][...] * (1.0 + sc2)) + sh2
    hb_scr[...] = h.astype(BF16)
    slot["x"][...] = x
    gate_scr[HIST:HIST + tok, :] = _dot(hb_scr[...], w["wg"][...])
    val = _dot(hb_scr[...], w["wval"][...])
    acc = _causal_conv(gate_scr, tok, w["fcw"], w["fcb"], FF_CONV)
    gate_scr[0:HIST, :] = gate_scr[tok:tok + HIST, :]
    slot["act"][...] = (_silu(acc) * val).astype(BF16)


def _ffn_down_stage(slot, mod_ref, w, o_ref, final):
    g2 = mod_ref[0, 0][5:6]
    xo = slot["x"][...] + g2 * _dot(slot["act"][...], w["wd"][...])
    if final:
        xo = (xo * _rms_scale(xo)) * w["fg"][...]
    o_ref[0] = xo


_FFN_CONSTS = ("n2g", "wg", "wval", "fcw", "fcb", "wd", "fg")
_FFN_FIELDS = (("x", (TOK, D_MODEL), F32), ("act", (TOK, D_FF), BF16))


def _ffn_kernel(*refs, tiles_per_seq, final):
    n_const = len(_FFN_CONSTS)
    x_ref, mod_in_ref, mod_out_ref = refs[:3]
    w = {name: ref.at[0] for name, ref in zip(_FFN_CONSTS, refs[3:3 + n_const])}
    o_ref = refs[3 + n_const]
    gate_scr, hb_scr = refs[4 + n_const:6 + n_const]
    slots = _slot_dicts(refs[6 + n_const:], _FFN_FIELDS)
    k = pl.program_id(0)

    @pl.when(k == 0)
    def _():
        _zero_slot(slots[1])

    @pl.when(k % tiles_per_seq == 0)
    def _():
        gate_scr[0:HIST, :] = jnp.zeros((HIST, D_FF), F32)

    def step(p):
        _ffn_down_stage(slots[1 - p], mod_out_ref, w, o_ref, final)
        _ffn_up_stage(x_ref, mod_in_ref, w, gate_scr, hb_scr, slots[p])

    @pl.when(k % 2 == 0)
    def _():
        step(0)

    @pl.when(k % 2 == 1)
    def _():
        step(1)


def _ffn(x, mod, p, l, final):
    bsz, seq, _ = x.shape
    tps = seq // TOK
    n_tiles = bsz * tps
    half_up = (D_MODEL, D_FF)
    operands = [(p["n2g"], _layer_spec(p["n2g"], l)),
                (p["wup"], _layer_spec(p["wup"], l, half_up, (0, 0))),
                (p["wup"], _layer_spec(p["wup"], l, half_up, (0, 1))),
                (p["fcw"], _layer_spec(p["fcw"], l)),
                (p["fcb"], _layer_spec(p["fcb"], l)),
                (p["wd"], _layer_spec(p["wd"], l)),
                (p["fg"], _layer_spec(p["fg"], 0))]

    def in_tile(k):
        t = jnp.minimum(k, n_tiles - 1)
        return t // tps, t % tps

    def out_tile(k):
        t = jnp.maximum(k - 1, 0)
        return t // tps, t % tps

    return pl.pallas_call(
        functools.partial(_ffn_kernel, tiles_per_seq=tps, final=final),
        grid=(n_tiles + 1,),
        in_specs=[
            pl.BlockSpec((1, TOK, D_MODEL), lambda k: (*in_tile(k), 0)),
            pl.BlockSpec((1, 1, N_MOD, D_MODEL),
                         lambda k: (l, in_tile(k)[0], 0, 0)),
            pl.BlockSpec((1, 1, N_MOD, D_MODEL),
                         lambda k: (l, out_tile(k)[0], 0, 0)),
        ] + [spec for _, spec in operands],
        out_specs=pl.BlockSpec((1, TOK, D_MODEL), lambda k: (*out_tile(k), 0)),
        out_shape=jax.ShapeDtypeStruct(x.shape, F32),
        scratch_shapes=[pltpu.VMEM((HIST + TOK, D_FF), F32),
                        pltpu.VMEM((TOK, D_MODEL), BF16)]
        + 2 * _slot_shapes(_FFN_FIELDS),
        compiler_params=pltpu.CompilerParams(
            dimension_semantics=("arbitrary",),
            vmem_limit_bytes=VMEM_LIMIT),
        name="ffn_final" if final else "ffn",
    )(x, mod, mod, *[a for a, _ in operands])


def _prep_params(norm1_g, norm2_g, w_in, ssd_conv_w, ssd_conv_b, ssd_dt_bias,
                 ssd_a_log, ssd_d, ssd_norm_g, gm_vnorm_g, gm_ws, gm_bs,
                 gm_out_g, w_out, ff_up, ff_conv_w, ff_conv_b, ff_down, final_g):
    o_dt = D_SSD + CONV_DIM
    o_u = o_dt + SSD_HEADS
    row = lambda a: a[:, None, :]
    col = lambda a: a[:, :, None]
    return {
        "n1g": row(norm1_g),
        "win": jnp.concatenate([w_in[:, :, :o_dt], w_in[:, :, o_u:]],
                               axis=2).astype(BF16),
        "wdtt": jnp.swapaxes(w_in[:, :, o_dt:o_u], 1, 2).astype(BF16),
        "cw": ssd_conv_w,
        "cb": row(ssd_conv_b),
        "dtb": col(ssd_dt_bias),
        "alog": col(ssd_a_log),
        "dsk": row(jnp.repeat(ssd_d, SSD_HEAD_DIM, axis=1)),
        "ssdg": row(ssd_norm_g),
        "vng": row(gm_vnorm_g),
        "ws": gm_ws,
        "bsx": jnp.repeat(jnp.swapaxes(gm_bs, 1, 2), GM_HEAD_DIM, axis=2),
        "gog": row(gm_out_g),
        "wo": w_out.astype(BF16),
        "n2g": row(norm2_g),
        "wup": ff_up.astype(BF16),
        "fcw": ff_conv_w,
        "fcb": row(ff_conv_b),
        "wd": ff_down.astype(BF16),
        "fg": final_g.reshape(1, 1, D_MODEL),
    }


def kernel(x, c, ada_w, ada_b, norm1_g, norm2_g, w_in, ssd_conv_w, ssd_conv_b, ssd_dt_bias, ssd_a_log, ssd_d, ssd_norm_g, gm_vnorm_g, gm_ws, gm_bs, gm_out_g, w_out, ff_up, ff_conv_w, ff_conv_b, ff_down, final_g):
    depth = ada_w.shape[0]
    bsz = x.shape[0]
    mod = _ada(c, ada_w, ada_b).reshape(depth, bsz, N_MOD, D_MODEL)
    p = _prep_params(norm1_g, norm2_g, w_in, ssd_conv_w, ssd_conv_b, ssd_dt_bias,
                     ssd_a_log, ssd_d, ssd_norm_g, gm_vnorm_g, gm_ws, gm_bs,
                     gm_out_g, w_out, ff_up, ff_conv_w, ff_conv_b, ff_down,
                     final_g)
    for l in range(depth):
        x = _mixer(x, mod, p, l)
        x = _ffn(x, mod, p, l, final=(l == depth - 1))
    return x
```

```python
import functools

import jax
import jax.numpy as jnp
from jax import lax
from jax.experimental import pallas as pl
from jax.experimental.pallas import tpu as pltpu

F32 = jnp.float32
BF16 = jnp.bfloat16

D_MODEL = 1024
SSD_HEAD_DIM = 64
SSD_HEADS = 16
D_SSD = 1024
SSD_GROUPS = 2
HEADS_PER_GROUP = 8
D_STATE = 128
SSD_CONV = 4
CHUNK = 128
CONV_DIM = D_SSD + 2 * SSD_GROUPS * D_STATE
GM_HEAD_DIM = 128
GM_HEADS = 8
D_GM = 1024
D_FF = 2816
FF_CONV = 3
N_MOD = 6
EPS = 1e-6
LOG2E = 1.4426950408889634

LANES = 128
SUBLANES = 8
HIST = SUBLANES
TOK = 256
FFN_TOK = 512
N_CHUNKS = TOK // CHUNK
VMEM_LIMIT = 56 * 1024 * 1024
GROUP_W = HEADS_PER_GROUP * SSD_HEAD_DIM


def _dot(a, b):
    return jnp.dot(a, b, preferred_element_type=F32)


def _dot_nt(a, b):
    return lax.dot_general(a, b, (((1,), (1,)), ((), ())),
                           preferred_element_type=F32)


def _silu(x):
    half = 0.5 * x
    return half + half * jnp.tanh(half)


def _gelu(x):
    return 0.5 * x * (1.0 + lax.erf(x * 0.7071067811865476))


def _rms_scale(x):
    return lax.rsqrt(jnp.mean(x * x, axis=-1, keepdims=True) + EPS)


def _split3(x):
    hi = x.astype(BF16)
    r1 = x - hi.astype(F32)
    mid = r1.astype(BF16)
    lo = (r1 - mid.astype(F32)).astype(BF16)
    return hi, mid, lo


def _causal_conv(scr, tok, w_ref, b_ref, width):
    full = scr[0:HIST + tok, :]
    acc = b_ref[...] + w_ref[width - 1:width, :] * full[HIST:]
    for j in range(1, width):
        acc = acc + w_ref[width - 1 - j:width - j, :] * pltpu.roll(full, j, 0)[HIST:]
    return acc


def _slot_shapes(fields):
    return [pltpu.VMEM(shape, dt) for _, shape, dt in fields]


def _slot_dicts(refs, fields):
    n = len(fields)
    return [dict(zip([f[0] for f in fields], refs[i * n:(i + 1) * n]))
            for i in range(2)]


def _zero_slot(slot):
    for ref in slot.values():
        ref[...] = jnp.zeros(ref.shape, ref.dtype)


def _layer_spec(stacked, l, block=None, index=None):
    block = (1,) + (stacked.shape[1:] if block is None else block)
    index = (l,) + ((0,) * (len(block) - 1) if index is None else index)
    return pl.BlockSpec(block, lambda *_: index, pipeline_mode=pl.Buffered(1))


def _ada_kernel(c_ref, w_ref, b_ref, o_ref):
    ca = _silu(c_ref[...]).astype(BF16)
    o_ref[0] = _dot(ca, w_ref[0].astype(BF16)) + b_ref[0]


def _ada(c, ada_w, ada_b):
    n_layers, _, n_out = ada_w.shape
    bsz = c.shape[0]
    tn = 1536
    return pl.pallas_call(
        _ada_kernel,
        grid=(n_layers, n_out // tn),
        in_specs=[
            pl.BlockSpec((bsz, D_MODEL), lambda l, j: (0, 0)),
            pl.BlockSpec((1, D_MODEL, tn), lambda l, j: (l, 0, j)),
            pl.BlockSpec((1, 1, tn), lambda l, j: (l, 0, j)),
        ],
        out_specs=pl.BlockSpec((1, bsz, tn), lambda l, j: (l, 0, j)),
        out_shape=jax.ShapeDtypeStruct((n_layers, bsz, n_out), F32),
        compiler_params=pltpu.CompilerParams(
            dimension_semantics=("arbitrary", "arbitrary"),
            vmem_limit_bytes=VMEM_LIMIT),
        name="ada_mod",
    )(c, ada_w, ada_b.reshape(n_layers, 1, n_out))


_PROJ_FIELDS = (
    ("x", (TOK, D_MODEL), F32),
    ("z", (TOK, D_SSD), F32),
    ("xs", (TOK, D_SSD), F32),
    ("bc", (TOK, 2 * SSD_GROUPS * D_STATE), F32),
    ("dt", (SSD_HEADS, TOK), F32),
    ("u", (TOK, D_GM), F32),
    ("vn", (TOK, D_GM), BF16),
)
_DECAY_FIELDS = (
    ("x", (TOK, D_MODEL), F32),
    ("sz", (TOK, D_SSD), F32),
    ("dsx", (TOK, D_SSD), F32),
    ("xsb", (TOK, D_SSD), BF16),
    ("gl", (TOK, D_GM), F32),
    ("lhs", (N_CHUNKS * SSD_HEADS, CHUNK, CHUNK + D_STATE), BF16),
    ("xw", (TOK, D_SSD), BF16),
    ("bt", (N_CHUNKS * SSD_GROUPS, D_STATE, CHUNK), BF16),
    ("dec", (N_CHUNKS * SSD_GROUPS, SUBLANES, GROUP_W), F32),
)


def _project_stage(x_ref, mod_ref, w, xbc_scr, hb_scr, slot):
    tok = x_ref.shape[1]
    x = x_ref[0]
    mod = mod_ref[0, 0]
    sh1, sc1 = mod[0:1], mod[1:2]
    h = (x * _rms_scale(x)) * (w["n1g"][...] * (1.0 + sc1)) + sh1
    hb_scr[...] = h.astype(BF16)
    slot["x"][...] = x
    proj = _dot(hb_scr[...], w["win"][...])
    slot["z"][...] = proj[:, :D_SSD]
    xbc_scr[HIST:HIST + tok, :] = proj[:, D_SSD:D_SSD + CONV_DIM]
    u_raw = proj[:, D_SSD + CONV_DIM:D_SSD + CONV_DIM + D_GM]
    v_raw = proj[:, D_SSD + CONV_DIM + D_GM:]
    slot["dt"][...] = _dot_nt(w["wdtt"][...], hb_scr[...])
    acc = _causal_conv(xbc_scr, tok, w["cw"], w["cb"], SSD_CONV)
    xbc_scr[0:HIST, :] = xbc_scr[tok:tok + HIST, :]
    xbc = _silu(acc)
    slot["xs"][...] = xbc[:, :D_SSD]
    slot["bc"][...] = xbc[:, D_SSD:]
    slot["u"][...] = _gelu(u_raw)
    v = _gelu(v_raw)
    slot["vn"][...] = ((v * _rms_scale(v)) * w["vng"][...]).astype(BF16)


def _decay_stage(src, w, slot):
    tok = src["xs"].shape[0]
    n_chunks = tok // CHUNK
    xs = src["xs"][...]
    bc = src["bc"][...]
    bmat = bc[:, :SSD_GROUPS * D_STATE]
    cmat = bc[:, SSD_GROUPS * D_STATE:]
    slot["x"][...] = src["x"][...]
    slot["sz"][...] = _silu(src["z"][...])
    slot["xsb"][...] = xs.astype(BF16)
    slot["dsx"][...] = w["dsk"][...] * xs

    dt_in = src["dt"][...] + w["dtb"][...]
    dt_t = jnp.maximum(dt_in, 0.0) + jnp.log1p(jnp.exp(-jnp.abs(dt_in)))
    dta_t = dt_t * (-jnp.exp(w["alog"][...]))
    log2_dt_t = jnp.log(dt_t) * LOG2E

    li = lax.broadcasted_iota(jnp.int32, (CHUNK, CHUNK), 0)
    si = lax.broadcasted_iota(jnp.int32, (CHUNK, CHUNK), 1)
    causal = li >= si
    triu_b = jnp.where(li <= si, 1.0, 0.0).astype(BF16)
    low_half = si[0:1] < SSD_HEAD_DIM

    for c in range(tok // CHUNK):
        r0 = c * CHUNK
        hi, mid, lo = _split3(dta_t[:, r0:r0 + CHUNK])
        a2_t = (_dot(hi, triu_b) + _dot(mid, triu_b) + _dot(lo, triu_b)) * LOG2E
        dt_c = dt_t[:, r0:r0 + CHUNK]
        row_t = a2_t - log2_dt_t[:, r0:r0 + CHUNK]
        w_t = dt_c * jnp.exp2(a2_t[:, CHUNK - 1:CHUNK] - a2_t)
        for g in range(SSD_GROUPS):
            b_blk = bmat[r0:r0 + CHUNK, g * D_STATE:(g + 1) * D_STATE]
            c_blk = cmat[r0:r0 + CHUNK, g * D_STATE:(g + 1) * D_STATE]
            cb = _dot_nt(c_blk.astype(BF16), b_blk.astype(BF16))
            slot["bt"][c * SSD_GROUPS + g] = b_blk.T.astype(BF16)
            w_parts, dec_parts = [], []
            for j in range(HEADS_PER_GROUP // 2):
                w_pair, e_pair = [], []
                for hd in (g * HEADS_PER_GROUP + 2 * j,
                           g * HEADS_PER_GROUP + 2 * j + 1):
                    col = jnp.broadcast_to(a2_t[hd:hd + 1, :], (CHUNK, CHUNK)).T
                    seg = jnp.where(causal, col - row_t[hd:hd + 1, :], -jnp.inf)
                    att = cb * jnp.exp2(seg)
                    e_col = jnp.exp2(col)
                    cd = c_blk * e_col
                    slot["lhs"][c * SSD_HEADS + hd] = jnp.concatenate(
                        [att.astype(BF16), cd.astype(BF16)], axis=1)
                    w_pair.append(jnp.broadcast_to(
                        w_t[hd:hd + 1, :], (SSD_HEAD_DIM, CHUNK)))
                    e_pair.append(e_col[CHUNK - 1:CHUNK, :])
                w_parts.append(jnp.concatenate(w_pair, axis=0).T)
                dec_parts.append(jnp.where(low_half, e_pair[0], e_pair[1]))
            w_exp = jnp.concatenate(w_parts, axis=1)
            slot["xw"][r0:r0 + CHUNK, g * GROUP_W:(g + 1) * GROUP_W] = (
                xs[r0:r0 + CHUNK, g * GROUP_W:(g + 1) * GROUP_W] * w_exp
            ).astype(BF16)
            slot["dec"][c * SSD_GROUPS + g, 0:1, :] = jnp.concatenate(
                dec_parts, axis=1)

    vn = src["vn"][...]
    sv_heads = []
    for hd in range(GM_HEADS):
        ws_m = jnp.where(causal, w["ws"][hd], 0.0).astype(BF16)
        v_h = jnp.concatenate(
            [vn[c * CHUNK:(c + 1) * CHUNK,
                hd * GM_HEAD_DIM:(hd + 1) * GM_HEAD_DIM]
             for c in range(n_chunks)], axis=1)
        sv_heads.append(_dot(ws_m, v_h))
    sv = jnp.concatenate(
        [jnp.concatenate([s[:, c * CHUNK:(c + 1) * CHUNK] for s in sv_heads],
                         axis=1) + w["bsx"][...]
         for c in range(n_chunks)], axis=0)
    slot["gl"][...] = src["u"][...] * sv


def _scan_out_stage(slot, mod_ref, w, state_scr, mix_scr, o_ref):
    tok = slot["sz"].shape[0]
    n_chunks = tok // CHUNK
    low_half = lax.broadcasted_iota(jnp.int32, (CHUNK, CHUNK), 1) < SSD_HEAD_DIM

    y_rows = []
    for c in range(n_chunks):
        r0 = c * CHUNK
        y_parts = []
        for g in range(SSD_GROUPS):
            state = state_scr[g]
            rhs_g = jnp.concatenate(
                [slot["xsb"][r0:r0 + CHUNK, g * GROUP_W:(g + 1) * GROUP_W],
                 state.astype(BF16)], axis=0)
            for j in range(HEADS_PER_GROUP // 2):
                rhs = rhs_g[:, j * LANES:(j + 1) * LANES]
                hd = c * SSD_HEADS + g * HEADS_PER_GROUP + 2 * j
                y_even = _dot(slot["lhs"][hd], rhs)
                y_odd = _dot(slot["lhs"][hd + 1], rhs)
                y_parts.append(jnp.where(low_half, y_even, y_odd))
            dec = slot["dec"][c * SSD_GROUPS + g, 0:1, :]
            state_scr[g] = state * dec + _dot(
                slot["bt"][c * SSD_GROUPS + g],
                slot["xw"][r0:r0 + CHUNK, g * GROUP_W:(g + 1) * GROUP_W])
        y_rows.append(jnp.concatenate(y_parts, axis=1))
    y_ssd = jnp.concatenate(y_rows, axis=0)

    yg_all = (y_ssd + slot["dsx"][...]) * slot["sz"][...]
    for g in range(SSD_GROUPS):
        yg = yg_all[:, g * GROUP_W:(g + 1) * GROUP_W]
        mix_scr[:, g * GROUP_W:(g + 1) * GROUP_W] = (
            (yg * _rms_scale(yg)) * w["ssdg"][:, g * GROUP_W:(g + 1) * GROUP_W]
        ).astype(BF16)
    gl = slot["gl"][...]
    mix_scr[:, D_SSD:] = ((gl * _rms_scale(gl)) * w["gog"][...]).astype(BF16)
    out = _dot(mix_scr[...], w["wo"][...])
    g1 = mod_ref[0, 0][2:3]
    o_ref[0] = slot["x"][...] + g1 * out


_MIXER_CONSTS = ("n1g", "win", "wdtt", "cw", "cb", "dtb",
                 "alog", "dsk", "ssdg", "vng", "ws", "bsx", "gog", "wo")


def _mixer_kernel(*refs, tiles_per_seq):
    n_const = len(_MIXER_CONSTS)
    x_ref, mod_in_ref, mod_out_ref = refs[:3]
    w = {name: ref.at[0] for name, ref in zip(_MIXER_CONSTS, refs[3:3 + n_const])}
    o_ref = refs[3 + n_const]
    xbc_scr, state_scr, hb_scr, mix_scr = refs[4 + n_const:8 + n_const]
    n_proj = 2 * len(_PROJ_FIELDS)
    proj_slots = _slot_dicts(refs[8 + n_const:8 + n_const + n_proj], _PROJ_FIELDS)
    decay_slots = _slot_dicts(refs[8 + n_const + n_proj:], _DECAY_FIELDS)
    k = pl.program_id(0)

    @pl.when(k == 0)
    def _():
        _zero_slot(proj_slots[1])
        _zero_slot(decay_slots[0])

    @pl.when(k % tiles_per_seq == 0)
    def _():
        xbc_scr[0:HIST, :] = jnp.zeros((HIST, CONV_DIM), F32)

    @pl.when(jnp.logical_or(k <= 1, (k - 2) % tiles_per_seq == 0))
    def _():
        state_scr[...] = jnp.zeros(state_scr.shape, F32)

    def step(p):
        _scan_out_stage(decay_slots[p], mod_out_ref, w, state_scr, mix_scr, o_ref)
        _decay_stage(proj_slots[1 - p], w, decay_slots[1 - p])
        _project_stage(x_ref, mod_in_ref, w, xbc_scr, hb_scr, proj_slots[p])

    @pl.when(k % 2 == 0)
    def _():
        step(0)

    @pl.when(k % 2 == 1)
    def _():
        step(1)


def _mixer(x, mod, p, l):
    bsz, seq, _ = x.shape
    tps = seq // TOK
    n_tiles = bsz * tps
    consts = [p[name] for name in _MIXER_CONSTS]

    def in_tile(k):
        t = jnp.minimum(k, n_tiles - 1)
        return t // tps, t % tps

    def out_tile(k):
        t = jnp.maximum(k - 2, 0)
        return t // tps, t % tps

    return pl.pallas_call(
        functools.partial(_mixer_kernel, tiles_per_seq=tps),
        grid=(n_tiles + 2,),
        in_specs=[
            pl.BlockSpec((1, TOK, D_MODEL), lambda k: (*in_tile(k), 0)),
            pl.BlockSpec((1, 1, N_MOD, D_MODEL),
                         lambda k: (l, in_tile(k)[0], 0, 0)),
            pl.BlockSpec((1, 1, N_MOD, D_MODEL),
                         lambda k: (l, out_tile(k)[0], 0, 0)),
        ] + [_layer_spec(a, l) for a in consts],
        out_specs=pl.BlockSpec((1, TOK, D_MODEL), lambda k: (*out_tile(k), 0)),
        out_shape=jax.ShapeDtypeStruct(x.shape, F32),
        scratch_shapes=[
            pltpu.VMEM((HIST + TOK, CONV_DIM), F32),
            pltpu.VMEM((SSD_GROUPS, D_STATE, GROUP_W), F32),
            pltpu.VMEM((TOK, D_MODEL), BF16),
            pltpu.VMEM((TOK, D_SSD + D_GM), BF16),
        ] + 2 * _slot_shapes(_PROJ_FIELDS) + 2 * _slot_shapes(_DECAY_FIELDS),
        compiler_params=pltpu.CompilerParams(
            dimension_semantics=("arbitrary",),
            vmem_limit_bytes=VMEM_LIMIT),
        name="mixer",
    )(x, mod, mod, *consts)


def _ffn_up_stage(x_ref, mod_ref, w, gate_scr, hb_scr, slot):
    tok = x_ref.shape[1]
    x = x_ref[0]
    mod = mod_ref[0, 0]
    sh2, sc2 = mod[3:4], mod[4:5]
    h = (x * _rms_scale(x)) * (w["n2g"][...] * (1.0 + sc2)) + sh2
    hb_scr[...] = h.astype(BF16)
    slot["x"][...] = x
    gate_scr[HIST:HIST + tok, :] = _dot(hb_scr[...], w["wg"][...])
    val = _dot(hb_scr[...], w["wval"][...])
    acc = _causal_conv(gate_scr, tok, w["fcw"], w["fcb"], FF_CONV)
    gate_scr[0:HIST, :] = gate_scr[tok:tok + HIST, :]
    slot["act"][...] = (_silu(acc) * val).astype(BF16)


def _ffn_down_stage(slot, mod_ref, w, o_ref, final):
    g2 = mod_ref[0, 0][5:6]
    xo = slot["x"][...] + g2 * _dot(slot["act"][...], w["wd"][...])
    if final:
        xo = (xo * _rms_scale(xo)) * w["fg"][...]
    o_ref[0] = xo


_FFN_CONSTS = ("n2g", "wg", "wval", "fcw", "fcb", "wd", "fg")
_FFN_FIELDS = (("x", (FFN_TOK, D_MODEL), F32), ("act", (FFN_TOK, D_FF), BF16))


def _ffn_kernel(*refs, tiles_per_seq, final):
    n_const = len(_FFN_CONSTS)
    x_ref, mod_in_ref, mod_out_ref = refs[:3]
    w = {name: ref.at[0] for name, ref in zip(_FFN_CONSTS, refs[3:3 + n_const])}
    o_ref = refs[3 + n_const]
    gate_scr, hb_scr = refs[4 + n_const:6 + n_const]
    slots = _slot_dicts(refs[6 + n_const:], _FFN_FIELDS)
    k = pl.program_id(0)

    @pl.when(k == 0)
    def _():
        _zero_slot(slots[1])

    @pl.when(k % tiles_per_seq == 0)
    def _():
        gate_scr[0:HIST, :] = jnp.zeros((HIST, D_FF), F32)

    def step(p):
        _ffn_down_stage(slots[1 - p], mod_out_ref, w, o_ref, final)
        _ffn_up_stage(x_ref, mod_in_ref, w, gate_scr, hb_scr, slots[p])

    @pl.when(k % 2 == 0)
    def _():
        step(0)

    @pl.when(k % 2 == 1)
    def _():
        step(1)


def _ffn(x, mod, p, l, final):
    bsz, seq, _ = x.shape
    tps = seq // FFN_TOK
    n_tiles = bsz * tps
    half_up = (D_MODEL, D_FF)
    operands = [(p["n2g"], _layer_spec(p["n2g"], l)),
                (p["wup"], _layer_spec(p["wup"], l, half_up, (0, 0))),
                (p["wup"], _layer_spec(p["wup"], l, half_up, (0, 1))),
                (p["fcw"], _layer_spec(p["fcw"], l)),
                (p["fcb"], _layer_spec(p["fcb"], l)),
                (p["wd"], _layer_spec(p["wd"], l)),
                (p["fg"], _layer_spec(p["fg"], 0))]

    def in_tile(k):
        t = jnp.minimum(k, n_tiles - 1)
        return t // tps, t % tps

    def out_tile(k):
        t = jnp.maximum(k - 1, 0)
        return t // tps, t % tps

    return pl.pallas_call(
        functools.partial(_ffn_kernel, tiles_per_seq=tps, final=final),
        grid=(n_tiles + 1,),
        in_specs=[
            pl.BlockSpec((1, FFN_TOK, D_MODEL), lambda k: (*in_tile(k), 0)),
            pl.BlockSpec((1, 1, N_MOD, D_MODEL),
                         lambda k: (l, in_tile(k)[0], 0, 0)),
            pl.BlockSpec((1, 1, N_MOD, D_MODEL),
                         lambda k: (l, out_tile(k)[0], 0, 0)),
        ] + [spec for _, spec in operands],
        out_specs=pl.BlockSpec((1, FFN_TOK, D_MODEL), lambda k: (*out_tile(k), 0)),
        out_shape=jax.ShapeDtypeStruct(x.shape, F32),
        scratch_shapes=[pltpu.VMEM((HIST + FFN_TOK, D_FF), F32),
                        pltpu.VMEM((FFN_TOK, D_MODEL), BF16)]
        + 2 * _slot_shapes(_FFN_FIELDS),
        compiler_params=pltpu.CompilerParams(
            dimension_semantics=("arbitrary",),
            vmem_limit_bytes=VMEM_LIMIT),
        name="ffn_final" if final else "ffn",
    )(x, mod, mod, *[a for a, _ in operands])


def _prep_params(norm1_g, norm2_g, w_in, ssd_conv_w, ssd_conv_b, ssd_dt_bias,
                 ssd_a_log, ssd_d, ssd_norm_g, gm_vnorm_g, gm_ws, gm_bs,
                 gm_out_g, w_out, ff_up, ff_conv_w, ff_conv_b, ff_down, final_g):
    o_dt = D_SSD + CONV_DIM
    o_u = o_dt + SSD_HEADS
    row = lambda a: a[:, None, :]
    col = lambda a: a[:, :, None]
    return {
        "n1g": row(norm1_g),
        "win": jnp.concatenate([w_in[:, :, :o_dt], w_in[:, :, o_u:]],
                               axis=2).astype(BF16),
        "wdtt": jnp.swapaxes(lax.optimization_barrier(w_in[:, :, o_dt:o_u]),
                             1, 2).astype(BF16),
        "cw": ssd_conv_w,
        "cb": row(ssd_conv_b),
        "dtb": col(ssd_dt_bias),
        "alog": col(ssd_a_log),
        "dsk": row(jnp.repeat(ssd_d, SSD_HEAD_DIM, axis=1)),
        "ssdg": row(ssd_norm_g),
        "vng": row(gm_vnorm_g),
        "ws": gm_ws,
        "bsx": jnp.repeat(jnp.swapaxes(gm_bs, 1, 2), GM_HEAD_DIM, axis=2),
        "gog": row(gm_out_g),
        "wo": w_out.astype(BF16),
        "n2g": row(norm2_g),
        "wup": ff_up.astype(BF16),
        "fcw": ff_conv_w,
        "fcb": row(ff_conv_b),
        "wd": ff_down.astype(BF16),
        "fg": final_g.reshape(1, 1, D_MODEL),
    }


def kernel(x, c, ada_w, ada_b, norm1_g, norm2_g, w_in, ssd_conv_w, ssd_conv_b, ssd_dt_bias, ssd_a_log, ssd_d, ssd_norm_g, gm_vnorm_g, gm_ws, gm_bs, gm_out_g, w_out, ff_up, ff_conv_w, ff_conv_b, ff_down, final_g):
    depth = ada_w.shape[0]
    bsz = x.shape[0]
    mod = _ada(c, ada_w, ada_b).reshape(depth, bsz, N_MOD, D_MODEL)
    p = _prep_params(norm1_g, norm2_g, w_in, ssd_conv_w, ssd_conv_b, ssd_dt_bias,
                     ssd_a_log, ssd_d, ssd_norm_g, gm_vnorm_g, gm_ws, gm_bs,
                     gm_out_g, w_out, ff_up, ff_conv_w, ff_conv_b, ff_down,
                     final_g)
    for l in range(depth):
        x = _mixer(x, mod, p, l)
        x = _ffn(x, mod, p, l, final=(l == depth - 1))
    return x
```

```python
import functools

import jax
import jax.numpy as jnp
from jax import lax
from jax.experimental import pallas as pl
from jax.experimental.pallas import tpu as pltpu

F32 = jnp.float32
BF16 = jnp.bfloat16

D_MODEL = 1024
SSD_HEAD_DIM = 64
SSD_HEADS = 16
D_SSD = 1024
SSD_GROUPS = 2
HEADS_PER_GROUP = 8
D_STATE = 128
SSD_CONV = 4
CHUNK = 128
CONV_DIM = D_SSD + 2 * SSD_GROUPS * D_STATE
GM_HEAD_DIM = 128
GM_HEADS = 8
D_GM = 1024
D_FF = 2816
FF_CONV = 3
N_MOD = 6
EPS = 1e-6
LOG2E = 1.4426950408889634

LANES = 128
SUBLANES = 8
HIST = SUBLANES
TOK = 256
FFN_TOK = 512
N_CHUNKS = TOK // CHUNK
VMEM_LIMIT = 56 * 1024 * 1024
GROUP_W = HEADS_PER_GROUP * SSD_HEAD_DIM


def _dot(a, b):
    return jnp.dot(a, b, preferred_element_type=F32)


def _dot_nt(a, b):
    return lax.dot_general(a, b, (((1,), (1,)), ((), ())),
                           preferred_element_type=F32)


def _silu(x):
    half = 0.5 * x
    return half + half * jnp.tanh(half)


def _gelu(x):
    return 0.5 * x * (1.0 + lax.erf(x * 0.7071067811865476))


def _rms_scale(x):
    return lax.rsqrt(jnp.mean(x * x, axis=-1, keepdims=True) + EPS)


def _split3(x):
    hi = x.astype(BF16)
    r1 = x - hi.astype(F32)
    mid = r1.astype(BF16)
    lo = (r1 - mid.astype(F32)).astype(BF16)
    return hi, mid, lo


def _causal_conv(full, w_ref, b_ref, width):
    acc = b_ref[...] + w_ref[width - 1:width, :] * full[HIST:]
    for j in range(1, width):
        acc = acc + w_ref[width - 1 - j:width - j, :] * pltpu.roll(full, j, 0)[HIST:]
    return acc


def _slot_shapes(fields):
    return [pltpu.VMEM(shape, dt) for _, shape, dt in fields]


def _slot_dicts(refs, fields):
    n = len(fields)
    return [dict(zip([f[0] for f in fields], refs[i * n:(i + 1) * n]))
            for i in range(2)]


def _zero_slot(slot):
    for ref in slot.values():
        ref[...] = jnp.zeros(ref.shape, ref.dtype)


def _layer_spec(stacked, l, block=None, index=None):
    block = (1,) + (stacked.shape[1:] if block is None else block)
    index = (l,) + ((0,) * (len(block) - 1) if index is None else index)
    return pl.BlockSpec(block, lambda *_: index, pipeline_mode=pl.Buffered(1))


def _ada_kernel(c_ref, w_ref, b_ref, o_ref):
    ca = _silu(c_ref[...]).astype(BF16)
    o_ref[0] = _dot(ca, w_ref[0].astype(BF16)) + b_ref[0]


def _ada(c, ada_w, ada_b):
    n_layers, _, n_out = ada_w.shape
    bsz = c.shape[0]
    tn = 1536
    return pl.pallas_call(
        _ada_kernel,
        grid=(n_layers, n_out // tn),
        in_specs=[
            pl.BlockSpec((bsz, D_MODEL), lambda l, j: (0, 0)),
            pl.BlockSpec((1, D_MODEL, tn), lambda l, j: (l, 0, j)),
            pl.BlockSpec((1, 1, tn), lambda l, j: (l, 0, j)),
        ],
        out_specs=pl.BlockSpec((1, bsz, tn), lambda l, j: (l, 0, j)),
        out_shape=jax.ShapeDtypeStruct((n_layers, bsz, n_out), F32),
        compiler_params=pltpu.CompilerParams(
            dimension_semantics=("arbitrary", "arbitrary"),
            vmem_limit_bytes=VMEM_LIMIT),
        name="ada_mod",
    )(c, ada_w, ada_b.reshape(n_layers, 1, n_out))


_PROJ_FIELDS = (
    ("z", (TOK, D_SSD), F32),
    ("xs", (TOK, D_SSD), F32),
    ("bc", (TOK, 2 * SSD_GROUPS * D_STATE), F32),
    ("dt", (SSD_HEADS, TOK), F32),
    ("u", (TOK, D_GM), F32),
    ("vn", (TOK, D_GM), BF16),
)
_DECAY_FIELDS = (
    ("sz", (TOK, D_SSD), F32),
    ("dsx", (TOK, D_SSD), F32),
    ("xsb", (TOK, D_SSD), BF16),
    ("gl", (TOK, D_GM), F32),
    ("lhs", (N_CHUNKS * SSD_HEADS, CHUNK, CHUNK + D_STATE), BF16),
    ("xw", (TOK, D_SSD), BF16),
    ("bt", (N_CHUNKS * SSD_GROUPS, D_STATE, CHUNK), BF16),
    ("dec", (N_CHUNKS * SSD_GROUPS, SUBLANES, GROUP_W), F32),
)


def _project_stage(x_ref, mod_ref, w, xbc_scr, hb_scr, slot):
    tok = x_ref.shape[1]
    x = x_ref[0]
    mod = mod_ref[0, 0]
    sh1, sc1 = mod[0:1], mod[1:2]
    h = (x * _rms_scale(x)) * (w["n1g"][...] * (1.0 + sc1)) + sh1
    hb_scr[...] = h.astype(BF16)
    proj = _dot(hb_scr[...], w["win"][...])
    slot["z"][...] = proj[:, :D_SSD]
    xbc_scr[HIST:HIST + tok, :] = proj[:, D_SSD:D_SSD + CONV_DIM]
    u_raw = proj[:, D_SSD + CONV_DIM:D_SSD + CONV_DIM + D_GM]
    v_raw = proj[:, D_SSD + CONV_DIM + D_GM:]
    slot["dt"][...] = _dot_nt(w["wdtt"][...], hb_scr[...])
    acc = _causal_conv(xbc_scr[0:HIST + tok, :], w["cw"], w["cb"], SSD_CONV)
    xbc_scr[0:HIST, :] = xbc_scr[tok:tok + HIST, :]
    xbc = _silu(acc)
    slot["xs"][...] = xbc[:, :D_SSD]
    slot["bc"][...] = xbc[:, D_SSD:]
    slot["u"][...] = _gelu(u_raw)
    v = _gelu(v_raw)
    slot["vn"][...] = ((v * _rms_scale(v)) * w["vng"][...]).astype(BF16)


def _decay_stage(src, w, slot):
    tok = src["xs"].shape[0]
    n_chunks = tok // CHUNK
    xs = src["xs"][...]
    bc = src["bc"][...]
    bmat = bc[:, :SSD_GROUPS * D_STATE]
    cmat = bc[:, SSD_GROUPS * D_STATE:]
    slot["sz"][...] = _silu(src["z"][...])
    slot["xsb"][...] = xs.astype(BF16)
    slot["dsx"][...] = w["dsk"][...] * xs

    dt_in = src["dt"][...] + w["dtb"][...]
    dt_t = jnp.maximum(dt_in, 0.0) + jnp.log1p(jnp.exp(-jnp.abs(dt_in)))
    dta_t = dt_t * (-jnp.exp(w["alog"][...]))
    log2_dt_t = jnp.log(dt_t) * LOG2E

    li = lax.broadcasted_iota(jnp.int32, (CHUNK, CHUNK), 0)
    si = lax.broadcasted_iota(jnp.int32, (CHUNK, CHUNK), 1)
    causal = li >= si
    triu_b = jnp.where(li <= si, 1.0, 0.0).astype(BF16)
    low_half = si[0:1] < SSD_HEAD_DIM

    for c in range(tok // CHUNK):
        r0 = c * CHUNK
        hi, mid, lo = _split3(dta_t[:, r0:r0 + CHUNK])
        a2_t = (_dot(hi, triu_b) + _dot(mid, triu_b) + _dot(lo, triu_b)) * LOG2E
        dt_c = dt_t[:, r0:r0 + CHUNK]
        row_t = a2_t - log2_dt_t[:, r0:r0 + CHUNK]
        w_t = dt_c * jnp.exp2(a2_t[:, CHUNK - 1:CHUNK] - a2_t)
        for g in range(SSD_GROUPS):
            b_blk = bmat[r0:r0 + CHUNK, g * D_STATE:(g + 1) * D_STATE]
            c_blk = cmat[r0:r0 + CHUNK, g * D_STATE:(g + 1) * D_STATE]
            cb = _dot_nt(c_blk.astype(BF16), b_blk.astype(BF16))
            slot["bt"][c * SSD_GROUPS + g] = b_blk.T.astype(BF16)
            w_parts, dec_parts = [], []
            for j in range(HEADS_PER_GROUP // 2):
                w_pair, e_pair = [], []
                for hd in (g * HEADS_PER_GROUP + 2 * j,
                           g * HEADS_PER_GROUP + 2 * j + 1):
                    col = jnp.broadcast_to(a2_t[hd:hd + 1, :], (CHUNK, CHUNK)).T
                    seg = jnp.where(causal, col - row_t[hd:hd + 1, :], -jnp.inf)
                    att = cb * jnp.exp2(seg)
                    e_col = jnp.exp2(col)
                    cd = c_blk * e_col
                    slot["lhs"][c * SSD_HEADS + hd] = jnp.concatenate(
                        [att.astype(BF16), cd.astype(BF16)], axis=1)
                    w_pair.append(jnp.broadcast_to(
                        w_t[hd:hd + 1, :], (SSD_HEAD_DIM, CHUNK)))
                    e_pair.append(e_col[CHUNK - 1:CHUNK, :])
                w_parts.append(jnp.concatenate(w_pair, axis=0).T)
                dec_parts.append(jnp.where(low_half, e_pair[0], e_pair[1]))
            w_exp = jnp.concatenate(w_parts, axis=1)
            slot["xw"][r0:r0 + CHUNK, g * GROUP_W:(g + 1) * GROUP_W] = (
                xs[r0:r0 + CHUNK, g * GROUP_W:(g + 1) * GROUP_W] * w_exp
            ).astype(BF16)
            slot["dec"][c * SSD_GROUPS + g, 0:1, :] = jnp.concatenate(
                dec_parts, axis=1)

    vn = src["vn"][...]
    sv_heads = []
    for hd in range(GM_HEADS):
        ws_m = jnp.where(causal, w["ws"][hd], 0.0).astype(BF16)
        v_h = jnp.concatenate(
            [vn[c * CHUNK:(c + 1) * CHUNK,
                hd * GM_HEAD_DIM:(hd + 1) * GM_HEAD_DIM]
             for c in range(n_chunks)], axis=1)
        sv_heads.append(_dot(ws_m, v_h))
    sv = jnp.concatenate(
        [jnp.concatenate([s[:, c * CHUNK:(c + 1) * CHUNK] for s in sv_heads],
                         axis=1) + w["bsx"][...]
         for c in range(n_chunks)], axis=0)
    slot["gl"][...] = src["u"][...] * sv


def _scan_out_stage(slot, x_ref, mod_ref, w, state_scr, mix_scr, o_ref):
    tok = slot["sz"].shape[0]
    n_chunks = tok // CHUNK
    low_half = lax.broadcasted_iota(jnp.int32, (CHUNK, CHUNK), 1) < SSD_HEAD_DIM

    y_rows = []
    for c in range(n_chunks):
        r0 = c * CHUNK
        y_parts = []
        for g in range(SSD_GROUPS):
            state = state_scr[g]
            rhs_g = jnp.concatenate(
                [slot["xsb"][r0:r0 + CHUNK, g * GROUP_W:(g + 1) * GROUP_W],
                 state.astype(BF16)], axis=0)
            for j in range(HEADS_PER_GROUP // 2):
                rhs = rhs_g[:, j * LANES:(j + 1) * LANES]
                hd = c * SSD_HEADS + g * HEADS_PER_GROUP + 2 * j
                y_even = _dot(slot["lhs"][hd], rhs)
                y_odd = _dot(slot["lhs"][hd + 1], rhs)
                y_parts.append(jnp.where(low_half, y_even, y_odd))
            dec = slot["dec"][c * SSD_GROUPS + g, 0:1, :]
            state_scr[g] = state * dec + _dot(
                slot["bt"][c * SSD_GROUPS + g],
                slot["xw"][r0:r0 + CHUNK, g * GROUP_W:(g + 1) * GROUP_W])
        y_rows.append(jnp.concatenate(y_parts, axis=1))
    y_ssd = jnp.concatenate(y_rows, axis=0)

    yg_all = (y_ssd + slot["dsx"][...]) * slot["sz"][...]
    for g in range(SSD_GROUPS):
        yg = yg_all[:, g * GROUP_W:(g + 1) * GROUP_W]
        mix_scr[:, g * GROUP_W:(g + 1) * GROUP_W] = (
            (yg * _rms_scale(yg)) * w["ssdg"][:, g * GROUP_W:(g + 1) * GROUP_W]
        ).astype(BF16)
    gl = slot["gl"][...]
    mix_scr[:, D_SSD:] = ((gl * _rms_scale(gl)) * w["gog"][...]).astype(BF16)
    out = _dot(mix_scr[...], w["wo"][...])
    g1 = mod_ref[0, 0][2:3]
    o_ref[0] = x_ref[0] + g1 * out


_MIXER_CONSTS = ("n1g", "win", "wdtt", "cw", "cb", "dtb",
                 "alog", "dsk", "ssdg", "vng", "ws", "bsx", "gog", "wo")


def _mixer_kernel(*refs, tiles_per_seq):
    n_const = len(_MIXER_CONSTS)
    x_ref, x_out_ref, mod_in_ref, mod_out_ref = refs[:4]
    w = {name: ref.at[0] for name, ref in zip(_MIXER_CONSTS, refs[4:4 + n_const])}
    o_ref = refs[4 + n_const]
    xbc_scr, state_scr, hb_scr, mix_scr = refs[5 + n_const:9 + n_const]
    n_proj = 2 * len(_PROJ_FIELDS)
    proj_slots = _slot_dicts(refs[9 + n_const:9 + n_const + n_proj], _PROJ_FIELDS)
    decay_slots = _slot_dicts(refs[9 + n_const + n_proj:], _DECAY_FIELDS)
    k = pl.program_id(0)

    @pl.when(k == 0)
    def _():
        _zero_slot(proj_slots[1])
        _zero_slot(decay_slots[0])

    @pl.when(k % tiles_per_seq == 0)
    def _():
        xbc_scr[0:HIST, :] = jnp.zeros((HIST, CONV_DIM), F32)

    @pl.when(jnp.logical_or(k <= 1, (k - 2) % tiles_per_seq == 0))
    def _():
        state_scr[...] = jnp.zeros(state_scr.shape, F32)

    def step(p):
        _scan_out_stage(decay_slots[p], x_out_ref, mod_out_ref, w, state_scr,
                        mix_scr, o_ref)
        _decay_stage(proj_slots[1 - p], w, decay_slots[1 - p])
        _project_stage(x_ref, mod_in_ref, w, xbc_scr, hb_scr, proj_slots[p])

    @pl.when(k % 2 == 0)
    def _():
        step(0)

    @pl.when(k % 2 == 1)
    def _():
        step(1)


def _mixer(x, mod, p, l):
    bsz, seq, _ = x.shape
    tps = seq // TOK
    n_tiles = bsz * tps
    consts = [p[name] for name in _MIXER_CONSTS]

    def in_tile(k):
        t = jnp.minimum(k, n_tiles - 1)
        return t // tps, t % tps

    def out_tile(k):
        t = jnp.maximum(k - 2, 0)
        return t // tps, t % tps

    return pl.pallas_call(
        functools.partial(_mixer_kernel, tiles_per_seq=tps),
        grid=(n_tiles + 2,),
        in_specs=[
            pl.BlockSpec((1, TOK, D_MODEL), lambda k: (*in_tile(k), 0)),
            pl.BlockSpec((1, TOK, D_MODEL), lambda k: (*out_tile(k), 0)),
            pl.BlockSpec((1, 1, N_MOD, D_MODEL),
                         lambda k: (l, in_tile(k)[0], 0, 0)),
            pl.BlockSpec((1, 1, N_MOD, D_MODEL),
                         lambda k: (l, out_tile(k)[0], 0, 0)),
        ] + [_layer_spec(a, l) for a in consts],
        out_specs=pl.BlockSpec((1, TOK, D_MODEL), lambda k: (*out_tile(k), 0)),
        out_shape=jax.ShapeDtypeStruct(x.shape, F32),
        scratch_shapes=[
            pltpu.VMEM((HIST + TOK, CONV_DIM), F32),
            pltpu.VMEM((SSD_GROUPS, D_STATE, GROUP_W), F32),
            pltpu.VMEM((TOK, D_MODEL), BF16),
            pltpu.VMEM((TOK, D_SSD + D_GM), BF16),
        ] + 2 * _slot_shapes(_PROJ_FIELDS) + 2 * _slot_shapes(_DECAY_FIELDS),
        compiler_params=pltpu.CompilerParams(
            dimension_semantics=("arbitrary",),
            vmem_limit_bytes=VMEM_LIMIT),
        name="mixer",
    )(x, x, mod, mod, *consts)


def _ffn_up_stage(x_ref, mod_ref, w, gate_scr, hb_scr, slot):
    tok = x_ref.shape[1]
    x = x_ref[0]
    mod = mod_ref[0, 0]
    sh2, sc2 = mod[3:4], mod[4:5]
    h = (x * _rms_scale(x)) * (w["n2g"][...] * (1.0 + sc2)) + sh2
    hb_scr[...] = h.astype(BF16)
    slot["x"][...] = x
    gate_scr[HIST:HIST + tok, :] = _dot(hb_scr[...], w["wg"][...])
    val = _dot(hb_scr[...], w["wval"][...])
    acc = _causal_conv(gate_scr[0:HIST + tok, :], w["fcw"], w["fcb"], FF_CONV)
    gate_scr[0:HIST, :] = gate_scr[tok:tok + HIST, :]
    slot["act"][...] = (_silu(acc) * val).astype(BF16)


def _ffn_down_stage(slot, mod_ref, w, dst_ref):
    g2 = mod_ref[0, 0][5:6]
    dst_ref[...] = slot["x"][...] + g2 * _dot(slot["act"][...], w["wd"][...])


def _final_norm_stage(slot, w, o_ref):
    xo = slot["xo"][...]
    o_ref[0] = (xo * _rms_scale(xo)) * w["fg"][...]


_FFN_CONSTS = ("n2g", "wg", "wval", "fcw", "fcb", "wd", "fg")
_FFN_FIELDS = (("x", (FFN_TOK, D_MODEL), F32), ("act", (FFN_TOK, D_FF), BF16))
_FFN_OUT_FIELDS = (("xo", (FFN_TOK, D_MODEL), F32),)


def _ffn_kernel(*refs, tiles_per_seq, final):
    n_const = len(_FFN_CONSTS)
    x_ref, mod_in_ref, mod_down_ref = refs[:3]
    w = {name: ref.at[0] for name, ref in zip(_FFN_CONSTS, refs[3:3 + n_const])}
    o_ref = refs[3 + n_const]
    gate_scr, hb_scr = refs[4 + n_const:6 + n_const]
    n_slot = 2 * len(_FFN_FIELDS)
    slots = _slot_dicts(refs[6 + n_const:6 + n_const + n_slot], _FFN_FIELDS)
    out_slots = _slot_dicts(refs[6 + n_const + n_slot:], _FFN_OUT_FIELDS)
    k = pl.program_id(0)

    @pl.when(k == 0)
    def _():
        _zero_slot(slots[1])
        if final:
            _zero_slot(out_slots[0])

    @pl.when(k % tiles_per_seq == 0)
    def _():
        gate_scr[0:HIST, :] = jnp.zeros((HIST, D_FF), F32)

    def step(p):
        if final:
            _final_norm_stage(out_slots[p], w, o_ref)
            _ffn_down_stage(slots[1 - p], mod_down_ref, w, out_slots[1 - p]["xo"])
        else:
            _ffn_down_stage(slots[1 - p], mod_down_ref, w, o_ref.at[0])
        _ffn_up_stage(x_ref, mod_in_ref, w, gate_scr, hb_scr, slots[p])

    @pl.when(k % 2 == 0)
    def _():
        step(0)

    @pl.when(k % 2 == 1)
    def _():
        step(1)


def _ffn(x, mod, p, l, final):
    bsz, seq, _ = x.shape
    tps = seq // FFN_TOK
    n_tiles = bsz * tps
    lag = 2 if final else 1
    half_up = (D_MODEL, D_FF)
    operands = [(p["n2g"], _layer_spec(p["n2g"], l)),
                (p["wup"], _layer_spec(p["wup"], l, half_up, (0, 0))),
                (p["wup"], _layer_spec(p["wup"], l, half_up, (0, 1))),
                (p["fcw"], _layer_spec(p["fcw"], l)),
                (p["fcb"], _layer_spec(p["fcb"], l)),
                (p["wd"], _layer_spec(p["wd"], l)),
                (p["fg"], _layer_spec(p["fg"], 0))]

    def tile(k, behind):
        t = jnp.clip(k - behind, 0, n_tiles - 1)
        return t // tps, t % tps

    return pl.pallas_call(
        functools.partial(_ffn_kernel, tiles_per_seq=tps, final=final),
        grid=(n_tiles + lag,),
        in_specs=[
            pl.BlockSpec((1, FFN_TOK, D_MODEL), lambda k: (*tile(k, 0), 0)),
            pl.BlockSpec((1, 1, N_MOD, D_MODEL),
                         lambda k: (l, tile(k, 0)[0], 0, 0)),
            pl.BlockSpec((1, 1, N_MOD, D_MODEL),
                         lambda k: (l, tile(k, 1)[0], 0, 0)),
        ] + [spec for _, spec in operands],
        out_specs=pl.BlockSpec((1, FFN_TOK, D_MODEL),
                               lambda k: (*tile(k, lag), 0)),
        out_shape=jax.ShapeDtypeStruct(x.shape, F32),
        scratch_shapes=[pltpu.VMEM((HIST + FFN_TOK, D_FF), F32),
                        pltpu.VMEM((FFN_TOK, D_MODEL), BF16)]
        + 2 * _slot_shapes(_FFN_FIELDS)
        + (2 * _slot_shapes(_FFN_OUT_FIELDS) if final else []),
        compiler_params=pltpu.CompilerParams(
            dimension_semantics=("arbitrary",),
            vmem_limit_bytes=VMEM_LIMIT),
        name="ffn_final" if final else "ffn",
    )(x, mod, mod, *[a for a, _ in operands])


def _prep_params(norm1_g, norm2_g, w_in, ssd_conv_w, ssd_conv_b, ssd_dt_bias,
                 ssd_a_log, ssd_d, ssd_norm_g, gm_vnorm_g, gm_ws, gm_bs,
                 gm_out_g, w_out, ff_up, ff_conv_w, ff_conv_b, ff_down, final_g):
    o_dt = D_SSD + CONV_DIM
    o_u = o_dt + SSD_HEADS
    row = lambda a: a[:, None, :]
    col = lambda a: a[:, :, None]
    return {
        "n1g": row(norm1_g),
        "win": jnp.concatenate([w_in[:, :, :o_dt], w_in[:, :, o_u:]],
                               axis=2).astype(BF16),
        "wdtt": jnp.swapaxes(lax.optimization_barrier(w_in[:, :, o_dt:o_u]),
                             1, 2).astype(BF16),
        "cw": ssd_conv_w,
        "cb": row(ssd_conv_b),
        "dtb": col(ssd_dt_bias),
        "alog": col(ssd_a_log),
        "dsk": row(jnp.repeat(ssd_d, SSD_HEAD_DIM, axis=1)),
        "ssdg": row(ssd_norm_g),
        "vng": row(gm_vnorm_g),
        "ws": gm_ws,
        "bsx": jnp.repeat(jnp.swapaxes(gm_bs, 1, 2), GM_HEAD_DIM, axis=2),
        "gog": row(gm_out_g),
        "wo": w_out.astype(BF16),
        "n2g": row(norm2_g),
        "wup": ff_up.astype(BF16),
        "fcw": ff_conv_w,
        "fcb": row(ff_conv_b),
        "wd": ff_down.astype(BF16),
        "fg": final_g.reshape(1, 1, D_MODEL),
    }


def kernel(x, c, ada_w, ada_b, norm1_g, norm2_g, w_in, ssd_conv_w, ssd_conv_b, ssd_dt_bias, ssd_a_log, ssd_d, ssd_norm_g, gm_vnorm_g, gm_ws, gm_bs, gm_out_g, w_out, ff_up, ff_conv_w, ff_conv_b, ff_down, final_g):
    depth = ada_w.shape[0]
    bsz = x.shape[0]
    mod = _ada(c, ada_w, ada_b).reshape(depth, bsz, N_MOD, D_MODEL)
    p = _prep_params(norm1_g, norm2_g, w_in, ssd_conv_w, ssd_conv_b, ssd_dt_bias,
                     ssd_a_log, ssd_d, ssd_norm_g, gm_vnorm_g, gm_ws, gm_bs,
                     gm_out_g, w_out, ff_up, ff_conv_w, ff_conv_b, ff_down,
                     final_g)
    for l in range(depth):
        x = _mixer(x, mod, p, l)
        x = _ffn(x, mod, p, l, final=(l == depth - 1))
    return x
```

```python
import functools

import jax
import jax.numpy as jnp
from jax import lax
from jax.experimental import pallas as pl
from jax.experimental.pallas import tpu as pltpu

F32 = jnp.float32
BF16 = jnp.bfloat16

D_MODEL = 1024
SSD_HEAD_DIM = 64
SSD_HEADS = 16
D_SSD = 1024
SSD_GROUPS = 2
HEADS_PER_GROUP = 8
D_STATE = 128
SSD_CONV = 4
CHUNK = 128
CONV_DIM = D_SSD + 2 * SSD_GROUPS * D_STATE
GM_HEAD_DIM = 128
GM_HEADS = 8
D_GM = 1024
D_FF = 2816
FF_CONV = 3
N_MOD = 6
EPS = 1e-6
LOG2E = 1.4426950408889634

LANES = 128
SUBLANES = 8
HIST = SUBLANES
TOK = 256
FFN_TOK = 512
N_CHUNKS = TOK // CHUNK
VMEM_LIMIT = 56 * 1024 * 1024
GROUP_W = HEADS_PER_GROUP * SSD_HEAD_DIM


def _dot(a, b):
    return jnp.dot(a, b, preferred_element_type=F32)


def _dot_nt(a, b):
    return lax.dot_general(a, b, (((1,), (1,)), ((), ())),
                           preferred_element_type=F32)


def _silu(x):
    half = 0.5 * x
    return half + half * jnp.tanh(half)


def _gelu(x):
    return 0.5 * x * (1.0 + lax.erf(x * 0.7071067811865476))


def _rms_scale(x):
    return lax.rsqrt(jnp.mean(x * x, axis=-1, keepdims=True) + EPS)


def _split3(x):
    hi = x.astype(BF16)
    r1 = x - hi.astype(F32)
    mid = r1.astype(BF16)
    lo = (r1 - mid.astype(F32)).astype(BF16)
    return hi, mid, lo


def _causal_conv(full, w_ref, b_ref, width):
    acc = b_ref[...] + w_ref[width - 1:width, :] * full[HIST:]
    for j in range(1, width):
        acc = acc + w_ref[width - 1 - j:width - j, :] * pltpu.roll(full, j, 0)[HIST:]
    return acc


def _slot_shapes(fields):
    return [pltpu.VMEM(shape, dt) for _, shape, dt in fields]


def _slot_dicts(refs, fields):
    n = len(fields)
    return [dict(zip([f[0] for f in fields], refs[i * n:(i + 1) * n]))
            for i in range(2)]


def _zero_slot(slot):
    for ref in slot.values():
        ref[...] = jnp.zeros(ref.shape, ref.dtype)


def _layer_spec(stacked, l, block=None, index=None):
    block = (1,) + (stacked.shape[1:] if block is None else block)
    index = (l,) + ((0,) * (len(block) - 1) if index is None else index)
    return pl.BlockSpec(block, lambda *_: index, pipeline_mode=pl.Buffered(1))


def _ada_kernel(c_ref, w_ref, b_ref, o_ref):
    ca = _silu(c_ref[...]).astype(BF16)
    o_ref[0] = _dot(ca, w_ref[0].astype(BF16)) + b_ref[0]


def _ada(c, ada_w, ada_b):
    n_layers, _, n_out = ada_w.shape
    bsz = c.shape[0]
    tn = 1536
    return pl.pallas_call(
        _ada_kernel,
        grid=(n_layers, n_out // tn),
        in_specs=[
            pl.BlockSpec((bsz, D_MODEL), lambda l, j: (0, 0)),
            pl.BlockSpec((1, D_MODEL, tn), lambda l, j: (l, 0, j)),
            pl.BlockSpec((1, 1, tn), lambda l, j: (l, 0, j)),
        ],
        out_specs=pl.BlockSpec((1, bsz, tn), lambda l, j: (l, 0, j)),
        out_shape=jax.ShapeDtypeStruct((n_layers, bsz, n_out), F32),
        compiler_params=pltpu.CompilerParams(
            dimension_semantics=("arbitrary", "arbitrary"),
            vmem_limit_bytes=VMEM_LIMIT),
        name="ada_mod",
    )(c, ada_w, ada_b.reshape(n_layers, 1, n_out))


def _regroup_kernel(w_ref, o_ref):
    o_dt = D_SSD + CONV_DIM
    w = w_ref[0]
    o_ref[0, :, :o_dt] = w[:, :o_dt].astype(BF16)
    o_ref[0, :, o_dt:] = w[:, o_dt + SSD_HEADS:].astype(BF16)


def _regroup_win(w_in):
    n_layers, k_dim, n_in = w_in.shape
    rows = 256
    return pl.pallas_call(
        _regroup_kernel,
        grid=(n_layers, k_dim // rows),
        in_specs=[pl.BlockSpec((1, rows, n_in), lambda l, i: (l, i, 0))],
        out_specs=pl.BlockSpec((1, rows, n_in - SSD_HEADS), lambda l, i: (l, i, 0)),
        out_shape=jax.ShapeDtypeStruct((n_layers, k_dim, n_in - SSD_HEADS), BF16),
        compiler_params=pltpu.CompilerParams(
            dimension_semantics=("arbitrary", "arbitrary"),
            vmem_limit_bytes=VMEM_LIMIT),
        name="win_prep",
    )(w_in)


_PROJ_FIELDS = (
    ("z", (TOK, D_SSD), F32),
    ("xs", (TOK, D_SSD), F32),
    ("bc", (TOK, 2 * SSD_GROUPS * D_STATE), F32),
    ("dt", (SSD_HEADS, TOK), F32),
    ("u", (TOK, D_GM), F32),
    ("vn", (TOK, D_GM), BF16),
)
_DECAY_FIELDS = (
    ("sz", (TOK, D_SSD), F32),
    ("dsx", (TOK, D_SSD), F32),
    ("xsb", (TOK, D_SSD), BF16),
    ("gl", (TOK, D_GM), F32),
    ("lhs", (N_CHUNKS * SSD_HEADS, CHUNK, CHUNK + D_STATE), BF16),
    ("xw", (TOK, D_SSD), BF16),
    ("bt", (N_CHUNKS * SSD_GROUPS, D_STATE, CHUNK), BF16),
    ("dec", (N_CHUNKS * SSD_GROUPS, SUBLANES, GROUP_W), F32),
)


def _project_stage(x_ref, mod_ref, w, xbc_scr, hb_scr, slot):
    tok = x_ref.shape[1]
    x = x_ref[0]
    mod = mod_ref[0, 0]
    sh1, sc1 = mod[0:1], mod[1:2]
    h = (x * _rms_scale(x)) * (w["n1g"][...] * (1.0 + sc1)) + sh1
    hb_scr[...] = h.astype(BF16)
    proj = _dot(hb_scr[...], w["win"][...])
    slot["z"][...] = proj[:, :D_SSD]
    xbc_scr[HIST:HIST + tok, :] = proj[:, D_SSD:D_SSD + CONV_DIM]
    u_raw = proj[:, D_SSD + CONV_DIM:D_SSD + CONV_DIM + D_GM]
    v_raw = proj[:, D_SSD + CONV_DIM + D_GM:]
    slot["dt"][...] = _dot_nt(w["wdtt"][...], hb_scr[...])
    acc = _causal_conv(xbc_scr[0:HIST + tok, :], w["cw"], w["cb"], SSD_CONV)
    xbc_scr[0:HIST, :] = xbc_scr[tok:tok + HIST, :]
    xbc = _silu(acc)
    slot["xs"][...] = xbc[:, :D_SSD]
    slot["bc"][...] = xbc[:, D_SSD:]
    slot["u"][...] = _gelu(u_raw)
    v = _gelu(v_raw)
    slot["vn"][...] = ((v * _rms_scale(v)) * w["vng"][...]).astype(BF16)


def _decay_stage(src, w, slot):
    tok = src["xs"].shape[0]
    n_chunks = tok // CHUNK
    xs = src["xs"][...]
    bc = src["bc"][...]
    bmat = bc[:, :SSD_GROUPS * D_STATE]
    cmat = bc[:, SSD_GROUPS * D_STATE:]
    slot["sz"][...] = _silu(src["z"][...])
    slot["xsb"][...] = xs.astype(BF16)
    slot["dsx"][...] = w["dsk"][...] * xs

    dt_in = src["dt"][...] + w["dtb"][...]
    dt_t = jnp.maximum(dt_in, 0.0) + jnp.log1p(jnp.exp(-jnp.abs(dt_in)))
    dta_t = dt_t * (-jnp.exp(w["alog"][...]))
    log2_dt_t = jnp.log(dt_t) * LOG2E

    li = lax.broadcasted_iota(jnp.int32, (CHUNK, CHUNK), 0)
    si = lax.broadcasted_iota(jnp.int32, (CHUNK, CHUNK), 1)
    causal = li >= si
    triu_b = jnp.where(li <= si, 1.0, 0.0).astype(BF16)
    low_half = si[0:1] < SSD_HEAD_DIM

    for c in range(tok // CHUNK):
        r0 = c * CHUNK
        hi, mid, lo = _split3(dta_t[:, r0:r0 + CHUNK])
        a2_t = (_dot(hi, triu_b) + _dot(mid, triu_b) + _dot(lo, triu_b)) * LOG2E
        dt_c = dt_t[:, r0:r0 + CHUNK]
        row_t = a2_t - log2_dt_t[:, r0:r0 + CHUNK]
        w_t = dt_c * jnp.exp2(a2_t[:, CHUNK - 1:CHUNK] - a2_t)
        for g in range(SSD_GROUPS):
            b_blk = bmat[r0:r0 + CHUNK, g * D_STATE:(g + 1) * D_STATE]
            c_blk = cmat[r0:r0 + CHUNK, g * D_STATE:(g + 1) * D_STATE]
            cb = _dot_nt(c_blk.astype(BF16), b_blk.astype(BF16))
            slot["bt"][c * SSD_GROUPS + g] = b_blk.T.astype(BF16)
            w_parts, dec_parts = [], []
            for j in range(HEADS_PER_GROUP // 2):
                w_pair, e_pair = [], []
                for hd in (g * HEADS_PER_GROUP + 2 * j,
                           g * HEADS_PER_GROUP + 2 * j + 1):
                    col = jnp.broadcast_to(a2_t[hd:hd + 1, :], (CHUNK, CHUNK)).T
                    seg = jnp.where(causal, col - row_t[hd:hd + 1, :], -jnp.inf)
                    att = cb * jnp.exp2(seg)
                    e_col = jnp.exp2(col)
                    cd = c_blk * e_col
                    slot["lhs"][c * SSD_HEADS + hd] = jnp.concatenate(
                        [att.astype(BF16), cd.astype(BF16)], axis=1)
                    w_pair.append(jnp.broadcast_to(
                        w_t[hd:hd + 1, :], (SSD_HEAD_DIM, CHUNK)))
                    e_pair.append(e_col[CHUNK - 1:CHUNK, :])
                w_parts.append(jnp.concatenate(w_pair, axis=0).T)
                dec_parts.append(jnp.where(low_half, e_pair[0], e_pair[1]))
            w_exp = jnp.concatenate(w_parts, axis=1)
            slot["xw"][r0:r0 + CHUNK, g * GROUP_W:(g + 1) * GROUP_W] = (
                xs[r0:r0 + CHUNK, g * GROUP_W:(g + 1) * GROUP_W] * w_exp
            ).astype(BF16)
            slot["dec"][c * SSD_GROUPS + g, 0:1, :] = jnp.concatenate(
                dec_parts, axis=1)

    vn = src["vn"][...]
    sv_heads = []
    for hd in range(GM_HEADS):
        ws_m = jnp.where(causal, w["ws"][hd], 0.0).astype(BF16)
        v_h = jnp.concatenate(
            [vn[c * CHUNK:(c + 1) * CHUNK,
                hd * GM_HEAD_DIM:(hd + 1) * GM_HEAD_DIM]
             for c in range(n_chunks)], axis=1)
        sv_heads.append(_dot(ws_m, v_h))
    sv = jnp.concatenate(
        [jnp.concatenate([s[:, c * CHUNK:(c + 1) * CHUNK] for s in sv_heads],
                         axis=1) + w["bsx"][...]
         for c in range(n_chunks)], axis=0)
    slot["gl"][...] = src["u"][...] * sv


def _scan_out_stage(slot, x_ref, mod_ref, w, state_scr, mix_scr, o_ref):
    tok = slot["sz"].shape[0]
    n_chunks = tok // CHUNK
    low_half = lax.broadcasted_iota(jnp.int32, (CHUNK, CHUNK), 1) < SSD_HEAD_DIM

    y_rows = []
    for c in range(n_chunks):
        r0 = c * CHUNK
        y_parts = []
        for g in range(SSD_GROUPS):
            state = state_scr[g]
            rhs_g = jnp.concatenate(
                [slot["xsb"][r0:r0 + CHUNK, g * GROUP_W:(g + 1) * GROUP_W],
                 state.astype(BF16)], axis=0)
            for j in range(HEADS_PER_GROUP // 2):
                rhs = rhs_g[:, j * LANES:(j + 1) * LANES]
                hd = c * SSD_HEADS + g * HEADS_PER_GROUP + 2 * j
                y_even = _dot(slot["lhs"][hd], rhs)
                y_odd = _dot(slot["lhs"][hd + 1], rhs)
                y_parts.append(jnp.where(low_half, y_even, y_odd))
            dec = slot["dec"][c * SSD_GROUPS + g, 0:1, :]
            state_scr[g] = state * dec + _dot(
                slot["bt"][c * SSD_GROUPS + g],
                slot["xw"][r0:r0 + CHUNK, g * GROUP_W:(g + 1) * GROUP_W])
        y_rows.append(jnp.concatenate(y_parts, axis=1))
    y_ssd = jnp.concatenate(y_rows, axis=0)

    yg_all = (y_ssd + slot["dsx"][...]) * slot["sz"][...]
    for g in range(SSD_GROUPS):
        yg = yg_all[:, g * GROUP_W:(g + 1) * GROUP_W]
        mix_scr[:, g * GROUP_W:(g + 1) * GROUP_W] = (
            (yg * _rms_scale(yg)) * w["ssdg"][:, g * GROUP_W:(g + 1) * GROUP_W]
        ).astype(BF16)
    gl = slot["gl"][...]
    mix_scr[:, D_SSD:] = ((gl * _rms_scale(gl)) * w["gog"][...]).astype(BF16)
    out = _dot(mix_scr[...], w["wo"][...])
    g1 = mod_ref[0, 0][2:3]
    o_ref[0] = x_ref[0] + g1 * out


_MIXER_CONSTS = ("n1g", "win", "wdtt", "cw", "cb", "dtb",
                 "alog", "dsk", "ssdg", "vng", "ws", "bsx", "gog", "wo")


def _mixer_kernel(*refs, tiles_per_seq):
    n_const = len(_MIXER_CONSTS)
    x_ref, x_out_ref, mod_in_ref, mod_out_ref = refs[:4]
    w = {name: ref.at[0] for name, ref in zip(_MIXER_CONSTS, refs[4:4 + n_const])}
    o_ref = refs[4 + n_const]
    xbc_scr, state_scr, hb_scr, mix_scr = refs[5 + n_const:9 + n_const]
    n_proj = 2 * len(_PROJ_FIELDS)
    proj_slots = _slot_dicts(refs[9 + n_const:9 + n_const + n_proj], _PROJ_FIELDS)
    decay_slots = _slot_dicts(refs[9 + n_const + n_proj:], _DECAY_FIELDS)
    k = pl.program_id(0)

    @pl.when(k == 0)
    def _():
        _zero_slot(proj_slots[1])
        _zero_slot(decay_slots[0])

    @pl.when(k % tiles_per_seq == 0)
    def _():
        xbc_scr[0:HIST, :] = jnp.zeros((HIST, CONV_DIM), F32)

    @pl.when(jnp.logical_or(k <= 1, (k - 2) % tiles_per_seq == 0))
    def _():
        state_scr[...] = jnp.zeros(state_scr.shape, F32)

    def step(p):
        _scan_out_stage(decay_slots[p], x_out_ref, mod_out_ref, w, state_scr,
                        mix_scr, o_ref)
        _decay_stage(proj_slots[1 - p], w, decay_slots[1 - p])
        _project_stage(x_ref, mod_in_ref, w, xbc_scr, hb_scr, proj_slots[p])

    @pl.when(k % 2 == 0)
    def _():
        step(0)

    @pl.when(k % 2 == 1)
    def _():
        step(1)


def _mixer(x, mod, p, l):
    bsz, seq, _ = x.shape
    tps = seq // TOK
    n_tiles = bsz * tps
    consts = [p[name] for name in _MIXER_CONSTS]

    def in_tile(k):
        t = jnp.minimum(k, n_tiles - 1)
        return t // tps, t % tps

    def out_tile(k):
        t = jnp.maximum(k - 2, 0)
        return t // tps, t % tps

    return pl.pallas_call(
        functools.partial(_mixer_kernel, tiles_per_seq=tps),
        grid=(n_tiles + 2,),
        in_specs=[
            pl.BlockSpec((1, TOK, D_MODEL), lambda k: (*in_tile(k), 0)),
            pl.BlockSpec((1, TOK, D_MODEL), lambda k: (*out_tile(k), 0)),
            pl.BlockSpec((1, 1, N_MOD, D_MODEL),
                         lambda k: (l, in_tile(k)[0], 0, 0)),
            pl.BlockSpec((1, 1, N_MOD, D_MODEL),
                         lambda k: (l, out_tile(k)[0], 0, 0)),
        ] + [_layer_spec(a, l) for a in consts],
        out_specs=pl.BlockSpec((1, TOK, D_MODEL), lambda k: (*out_tile(k), 0)),
        out_shape=jax.ShapeDtypeStruct(x.shape, F32),
        scratch_shapes=[
            pltpu.VMEM((HIST + TOK, CONV_DIM), F32),
            pltpu.VMEM((SSD_GROUPS, D_STATE, GROUP_W), F32),
            pltpu.VMEM((TOK, D_MODEL), BF16),
            pltpu.VMEM((TOK, D_SSD + D_GM), BF16),
        ] + 2 * _slot_shapes(_PROJ_FIELDS) + 2 * _slot_shapes(_DECAY_FIELDS),
        compiler_params=pltpu.CompilerParams(
            dimension_semantics=("arbitrary",),
            vmem_limit_bytes=VMEM_LIMIT),
        name="mixer",
    )(x, x, mod, mod, *consts)


def _ffn_up_stage(x_ref, mod_ref, w, gate_scr, hb_scr, slot):
    tok = x_ref.shape[1]
    x = x_ref[0]
    mod = mod_ref[0, 0]
    sh2, sc2 = mod[3:4], mod[4:5]
    h = (x * _rms_scale(x)) * (w["n2g"][...] * (1.0 + sc2)) + sh2
    hb_scr[...] = h.astype(BF16)
    slot["x"][...] = x
    gate_scr[HIST:HIST + tok, :] = _dot(hb_scr[...], w["wg"][...])
    val = _dot(hb_scr[...], w["wval"][...])
    acc = _causal_conv(gate_scr[0:HIST + tok, :], w["fcw"], w["fcb"], FF_CONV)
    gate_scr[0:HIST, :] = gate_scr[tok:tok + HIST, :]
    slot["act"][...] = (_silu(acc) * val).astype(BF16)


def _ffn_down_stage(slot, mod_ref, w, dst_ref):
    g2 = mod_ref[0, 0][5:6]
    dst_ref[...] = slot["x"][...] + g2 * _dot(slot["act"][...], w["wd"][...])


def _final_norm_stage(slot, w, o_ref):
    xo = slot["xo"][...]
    o_ref[0] = (xo * _rms_scale(xo)) * w["fg"][...]


_FFN_CONSTS = ("n2g", "wg", "wval", "fcw", "fcb", "wd", "fg")
_FFN_FIELDS = (("x", (FFN_TOK, D_MODEL), F32), ("act", (FFN_TOK, D_FF), BF16))
_FFN_OUT_FIELDS = (("xo", (FFN_TOK, D_MODEL), F32),)


def _ffn_kernel(*refs, tiles_per_seq, final):
    n_const = len(_FFN_CONSTS)
    x_ref, mod_in_ref, mod_down_ref = refs[:3]
    w = {name: ref.at[0] for name, ref in zip(_FFN_CONSTS, refs[3:3 + n_const])}
    o_ref = refs[3 + n_const]
    gate_scr, hb_scr = refs[4 + n_const:6 + n_const]
    n_slot = 2 * len(_FFN_FIELDS)
    slots = _slot_dicts(refs[6 + n_const:6 + n_const + n_slot], _FFN_FIELDS)
    out_slots = _slot_dicts(refs[6 + n_const + n_slot:], _FFN_OUT_FIELDS)
    k = pl.program_id(0)

    @pl.when(k == 0)
    def _():
        _zero_slot(slots[1])
        if final:
            _zero_slot(out_slots[0])

    @pl.when(k % tiles_per_seq == 0)
    def _():
        gate_scr[0:HIST, :] = jnp.zeros((HIST, D_FF), F32)

    def step(p):
        if final:
            _final_norm_stage(out_slots[p], w, o_ref)
            _ffn_down_stage(slots[1 - p], mod_down_ref, w, out_slots[1 - p]["xo"])
        else:
            _ffn_down_stage(slots[1 - p], mod_down_ref, w, o_ref.at[0])
        _ffn_up_stage(x_ref, mod_in_ref, w, gate_scr, hb_scr, slots[p])

    @pl.when(k % 2 == 0)
    def _():
        step(0)

    @pl.when(k % 2 == 1)
    def _():
        step(1)


def _ffn(x, mod, p, l, final):
    bsz, seq, _ = x.shape
    tps = seq // FFN_TOK
    n_tiles = bsz * tps
    lag = 2 if final else 1
    half_up = (D_MODEL, D_FF)
    operands = [(p["n2g"], _layer_spec(p["n2g"], l)),
                (p["wup"], _layer_spec(p["wup"], l, half_up, (0, 0))),
                (p["wup"], _layer_spec(p["wup"], l, half_up, (0, 1))),
                (p["fcw"], _layer_spec(p["fcw"], l)),
                (p["fcb"], _layer_spec(p["fcb"], l)),
                (p["wd"], _layer_spec(p["wd"], l)),
                (p["fg"], _layer_spec(p["fg"], 0))]

    def tile(k, behind):
        t = jnp.clip(k - behind, 0, n_tiles - 1)
        return t // tps, t % tps

    return pl.pallas_call(
        functools.partial(_ffn_kernel, tiles_per_seq=tps, final=final),
        grid=(n_tiles + lag,),
        in_specs=[
            pl.BlockSpec((1, FFN_TOK, D_MODEL), lambda k: (*tile(k, 0), 0)),
            pl.BlockSpec((1, 1, N_MOD, D_MODEL),
                         lambda k: (l, tile(k, 0)[0], 0, 0)),
            pl.BlockSpec((1, 1, N_MOD, D_MODEL),
                         lambda k: (l, tile(k, 1)[0], 0, 0)),
        ] + [spec for _, spec in operands],
        out_specs=pl.BlockSpec((1, FFN_TOK, D_MODEL),
                               lambda k: (*tile(k, lag), 0)),
        out_shape=jax.ShapeDtypeStruct(x.shape, F32),
        scratch_shapes=[pltpu.VMEM((HIST + FFN_TOK, D_FF), F32),
                        pltpu.VMEM((FFN_TOK, D_MODEL), BF16)]
        + 2 * _slot_shapes(_FFN_FIELDS)
        + (2 * _slot_shapes(_FFN_OUT_FIELDS) if final else []),
        compiler_params=pltpu.CompilerParams(
            dimension_semantics=("arbitrary",),
            vmem_limit_bytes=VMEM_LIMIT),
        name="ffn_final" if final else "ffn",
    )(x, mod, mod, *[a for a, _ in operands])


def _prep_params(norm1_g, norm2_g, w_in, ssd_conv_w, ssd_conv_b, ssd_dt_bias,
                 ssd_a_log, ssd_d, ssd_norm_g, gm_vnorm_g, gm_ws, gm_bs,
                 gm_out_g, w_out, ff_up, ff_conv_w, ff_conv_b, ff_down, final_g):
    o_dt = D_SSD + CONV_DIM
    o_u = o_dt + SSD_HEADS
    row = lambda a: a[:, None, :]
    col = lambda a: a[:, :, None]
    return {
        "n1g": row(norm1_g),
        "win": _regroup_win(w_in),
        "wdtt": jnp.swapaxes(w_in[:, :, o_dt:o_u], 1, 2).astype(BF16),
        "cw": ssd_conv_w,
        "cb": row(ssd_conv_b),
        "dtb": col(ssd_dt_bias),
        "alog": col(ssd_a_log),
        "dsk": row(jnp.repeat(ssd_d, SSD_HEAD_DIM, axis=1)),
        "ssdg": row(ssd_norm_g),
        "vng": row(gm_vnorm_g),
        "ws": gm_ws,
        "bsx": jnp.repeat(jnp.swapaxes(gm_bs, 1, 2), GM_HEAD_DIM, axis=2),
        "gog": row(gm_out_g),
        "wo": w_out.astype(BF16),
        "n2g": row(norm2_g),
        "wup": ff_up.astype(BF16),
        "fcw": ff_conv_w,
        "fcb": row(ff_conv_b),
        "wd": ff_down.astype(BF16),
        "fg": final_g.reshape(1, 1, D_MODEL),
    }


def kernel(x, c, ada_w, ada_b, norm1_g, norm2_g, w_in, ssd_conv_w, ssd_conv_b, ssd_dt_bias, ssd_a_log, ssd_d, ssd_norm_g, gm_vnorm_g, gm_ws, gm_bs, gm_out_g, w_out, ff_up, ff_conv_w, ff_conv_b, ff_down, final_g):
    depth = ada_w.shape[0]
    bsz = x.shape[0]
    mod = _ada(c, ada_w, ada_b).reshape(depth, bsz, N_MOD, D_MODEL)
    p = _prep_params(norm1_g, norm2_g, w_in, ssd_conv_w, ssd_conv_b, ssd_dt_bias,
                     ssd_a_log, ssd_d, ssd_norm_g, gm_vnorm_g, gm_ws, gm_bs,
                     gm_out_g, w_out, ff_up, ff_conv_w, ff_conv_b, ff_down,
                     final_g)
    for l in range(depth):
        x = _mixer(x, mod, p, l)
        x = _ffn(x, mod, p, l, final=(l == depth - 1))
    return x
```

```python
import functools

import jax
import jax.numpy as jnp
from jax import lax
from jax.experimental import pallas as pl
from jax.experimental.pallas import tpu as pltpu

F32 = jnp.float32
BF16 = jnp.bfloat16

D_MODEL = 1024
SSD_HEAD_DIM = 64
SSD_HEADS = 16
D_SSD = 1024
SSD_GROUPS = 2
HEADS_PER_GROUP = 8
D_STATE = 128
SSD_CONV = 4
CHUNK = 128
CONV_DIM = D_SSD + 2 * SSD_GROUPS * D_STATE
GM_HEAD_DIM = 128
GM_HEADS = 8
D_GM = 1024
D_FF = 2816
FF_CONV = 3
N_MOD = 6
EPS = 1e-6
LOG2E = 1.4426950408889634

LANES = 128
SUBLANES = 8
HIST = SUBLANES
TOK = 256
FFN_TOK = 512
N_CHUNKS = TOK // CHUNK
VMEM_LIMIT = 56 * 1024 * 1024
GROUP_W = HEADS_PER_GROUP * SSD_HEAD_DIM


def _dot(a, b):
    return jnp.dot(a, b, preferred_element_type=F32)


def _dot_nt(a, b):
    return lax.dot_general(a, b, (((1,), (1,)), ((), ())),
                           preferred_element_type=F32)


def _silu(x):
    half = 0.5 * x
    return half + half * jnp.tanh(half)


def _gelu(x):
    return 0.5 * x * (1.0 + lax.erf(x * 0.7071067811865476))


def _rms_scale(x):
    return lax.rsqrt(jnp.mean(x * x, axis=-1, keepdims=True) + EPS)


def _split3(x):
    hi = x.astype(BF16)
    r1 = x - hi.astype(F32)
    mid = r1.astype(BF16)
    lo = (r1 - mid.astype(F32)).astype(BF16)
    return hi, mid, lo


def _causal_conv(full, w_ref, b_ref, width):
    acc = b_ref[...] + w_ref[width - 1:width, :] * full[HIST:]
    for j in range(1, width):
        acc = acc + w_ref[width - 1 - j:width - j, :] * pltpu.roll(full, j, 0)[HIST:]
    return acc


def _slot_shapes(fields):
    return [pltpu.VMEM(shape, dt) for _, shape, dt in fields]


def _slot_dicts(refs, fields):
    n = len(fields)
    return [dict(zip([f[0] for f in fields], refs[i * n:(i + 1) * n]))
            for i in range(2)]


def _zero_slot(slot):
    for ref in slot.values():
        ref[...] = jnp.zeros(ref.shape, ref.dtype)


def _layer_spec(stacked, l, block=None, index=None):
    block = (1,) + (stacked.shape[1:] if block is None else block)
    index = (l,) + ((0,) * (len(block) - 1) if index is None else index)
    return pl.BlockSpec(block, lambda *_: index, pipeline_mode=pl.Buffered(1))


def _ada_kernel(c_ref, w_ref, b_ref, o_ref):
    ca = _silu(c_ref[...]).astype(BF16)
    o_ref[0] = _dot(ca, w_ref[0].astype(BF16)) + b_ref[0]


def _ada(c, ada_w, ada_b):
    n_layers, _, n_out = ada_w.shape
    bsz = c.shape[0]
    tn = 1536
    return pl.pallas_call(
        _ada_kernel,
        grid=(n_layers, n_out // tn),
        in_specs=[
            pl.BlockSpec((bsz, D_MODEL), lambda l, j: (0, 0)),
            pl.BlockSpec((1, D_MODEL, tn), lambda l, j: (l, 0, j)),
            pl.BlockSpec((1, 1, tn), lambda l, j: (l, 0, j)),
        ],
        out_specs=pl.BlockSpec((1, bsz, tn), lambda l, j: (l, 0, j)),
        out_shape=jax.ShapeDtypeStruct((n_layers, bsz, n_out), F32),
        compiler_params=pltpu.CompilerParams(
            dimension_semantics=("arbitrary", "arbitrary"),
            vmem_limit_bytes=VMEM_LIMIT),
        name="ada_mod",
    )(c, ada_w, ada_b.reshape(n_layers, 1, n_out))


PREP_COLS = 256
N_ZX_BLOCKS = (D_SSD + CONV_DIM) // PREP_COLS


def _regroup_kernel(a_ref, b_ref, o_ref):
    j = pl.program_id(1)
    a = a_ref[0]
    past_gap = jnp.concatenate([a[SSD_HEADS:], b_ref[0]], axis=0)
    src = jnp.where(j < N_ZX_BLOCKS, a, past_gap)
    o_ref[0] = src.T.astype(BF16)


def _regroup_win(w_in_t):
    n_layers, n_in, k_dim = w_in_t.shape
    n_out = n_in - SSD_HEADS
    return pl.pallas_call(
        _regroup_kernel,
        grid=(n_layers, n_out // PREP_COLS),
        in_specs=[
            pl.BlockSpec((1, PREP_COLS, k_dim), lambda l, j: (l, j, 0)),
            pl.BlockSpec((1, SSD_HEADS, k_dim),
                         lambda l, j: (l, (j + 1) * (PREP_COLS // SSD_HEADS), 0)),
        ],
        out_specs=pl.BlockSpec((1, k_dim, PREP_COLS), lambda l, j: (l, 0, j)),
        out_shape=jax.ShapeDtypeStruct((n_layers, k_dim, n_out), BF16),
        compiler_params=pltpu.CompilerParams(
            dimension_semantics=("arbitrary", "arbitrary"),
            vmem_limit_bytes=VMEM_LIMIT),
        name="win_prep",
    )(w_in_t, w_in_t)


_PROJ_FIELDS = (
    ("z", (TOK, D_SSD), F32),
    ("xs", (TOK, D_SSD), F32),
    ("bc", (TOK, 2 * SSD_GROUPS * D_STATE), F32),
    ("dt", (SSD_HEADS, TOK), F32),
    ("u", (TOK, D_GM), F32),
    ("vn", (TOK, D_GM), BF16),
)
_DECAY_FIELDS = (
    ("sz", (TOK, D_SSD), F32),
    ("dsx", (TOK, D_SSD), F32),
    ("xsb", (TOK, D_SSD), BF16),
    ("gl", (TOK, D_GM), F32),
    ("lhs", (N_CHUNKS * SSD_HEADS, CHUNK, CHUNK + D_STATE), BF16),
    ("xw", (TOK, D_SSD), BF16),
    ("bt", (N_CHUNKS * SSD_GROUPS, D_STATE, CHUNK), BF16),
    ("dec", (N_CHUNKS * SSD_GROUPS, SUBLANES, GROUP_W), F32),
)


def _project_stage(x_ref, mod_ref, w, xbc_scr, hb_scr, slot):
    tok = x_ref.shape[1]
    x = x_ref[0]
    mod = mod_ref[0, 0]
    sh1, sc1 = mod[0:1], mod[1:2]
    h = (x * _rms_scale(x)) * (w["n1g"][...] * (1.0 + sc1)) + sh1
    hb_scr[...] = h.astype(BF16)
    proj = _dot(hb_scr[...], w["win"][...])
    slot["z"][...] = proj[:, :D_SSD]
    xbc_scr[HIST:HIST + tok, :] = proj[:, D_SSD:D_SSD + CONV_DIM]
    u_raw = proj[:, D_SSD + CONV_DIM:D_SSD + CONV_DIM + D_GM]
    v_raw = proj[:, D_SSD + CONV_DIM + D_GM:]
    slot["dt"][...] = _dot_nt(w["wdtt"][...], hb_scr[...])
    acc = _causal_conv(xbc_scr[0:HIST + tok, :], w["cw"], w["cb"], SSD_CONV)
    xbc_scr[0:HIST, :] = xbc_scr[tok:tok + HIST, :]
    xbc = _silu(acc)
    slot["xs"][...] = xbc[:, :D_SSD]
    slot["bc"][...] = xbc[:, D_SSD:]
    slot["u"][...] = _gelu(u_raw)
    v = _gelu(v_raw)
    slot["vn"][...] = ((v * _rms_scale(v)) * w["vng"][...]).astype(BF16)


def _decay_stage(src, w, slot):
    tok = src["xs"].shape[0]
    n_chunks = tok // CHUNK
    xs = src["xs"][...]
    bc = src["bc"][...]
    bmat = bc[:, :SSD_GROUPS * D_STATE]
    cmat = bc[:, SSD_GROUPS * D_STATE:]
    slot["sz"][...] = _silu(src["z"][...])
    slot["xsb"][...] = xs.astype(BF16)
    slot["dsx"][...] = w["dsk"][...] * xs

    dt_in = src["dt"][...] + w["dtb"][...]
    dt_t = jnp.maximum(dt_in, 0.0) + jnp.log1p(jnp.exp(-jnp.abs(dt_in)))
    dta_t = dt_t * (-jnp.exp(w["alog"][...]))
    log2_dt_t = jnp.log(dt_t) * LOG2E

    li = lax.broadcasted_iota(jnp.int32, (CHUNK, CHUNK), 0)
    si = lax.broadcasted_iota(jnp.int32, (CHUNK, CHUNK), 1)
    causal = li >= si
    triu_b = jnp.where(li <= si, 1.0, 0.0).astype(BF16)
    low_half = si[0:1] < SSD_HEAD_DIM

    for c in range(tok // CHUNK):
        r0 = c * CHUNK
        hi, mid, lo = _split3(dta_t[:, r0:r0 + CHUNK])
        a2_t = (_dot(hi, triu_b) + _dot(mid, triu_b) + _dot(lo, triu_b)) * LOG2E
        dt_c = dt_t[:, r0:r0 + CHUNK]
        row_t = a2_t - log2_dt_t[:, r0:r0 + CHUNK]
        w_t = dt_c * jnp.exp2(a2_t[:, CHUNK - 1:CHUNK] - a2_t)
        for g in range(SSD_GROUPS):
            b_blk = bmat[r0:r0 + CHUNK, g * D_STATE:(g + 1) * D_STATE]
            c_blk = cmat[r0:r0 + CHUNK, g * D_STATE:(g + 1) * D_STATE]
            cb = _dot_nt(c_blk.astype(BF16), b_blk.astype(BF16))
            slot["bt"][c * SSD_GROUPS + g] = b_blk.T.astype(BF16)
            w_parts, dec_parts = [], []
            for j in range(HEADS_PER_GROUP // 2):
                w_pair, e_pair = [], []
                for hd in (g * HEADS_PER_GROUP + 2 * j,
                           g * HEADS_PER_GROUP + 2 * j + 1):
                    col = jnp.broadcast_to(a2_t[hd:hd + 1, :], (CHUNK, CHUNK)).T
                    seg = jnp.where(causal, col - row_t[hd:hd + 1, :], -jnp.inf)
                    att = cb * jnp.exp2(seg)
                    e_col = jnp.exp2(col)
                    cd = c_blk * e_col
                    slot["lhs"][c * SSD_HEADS + hd] = jnp.concatenate(
                        [att.astype(BF16), cd.astype(BF16)], axis=1)
                    w_pair.append(jnp.broadcast_to(
                        w_t[hd:hd + 1, :], (SSD_HEAD_DIM, CHUNK)))
                    e_pair.append(e_col[CHUNK - 1:CHUNK, :])
                w_parts.append(jnp.concatenate(w_pair, axis=0).T)
                dec_parts.append(jnp.where(low_half, e_pair[0], e_pair[1]))
            w_exp = jnp.concatenate(w_parts, axis=1)
            slot["xw"][r0:r0 + CHUNK, g * GROUP_W:(g + 1) * GROUP_W] = (
                xs[r0:r0 + CHUNK, g * GROUP_W:(g + 1) * GROUP_W] * w_exp
            ).astype(BF16)
            slot["dec"][c * SSD_GROUPS + g, 0:1, :] = jnp.concatenate(
                dec_parts, axis=1)

    vn = src["vn"][...]
    sv_heads = []
    for hd in range(GM_HEADS):
        ws_m = jnp.where(causal, w["ws"][hd], 0.0).astype(BF16)
        v_h = jnp.concatenate(
            [vn[c * CHUNK:(c + 1) * CHUNK,
                hd * GM_HEAD_DIM:(hd + 1) * GM_HEAD_DIM]
             for c in range(n_chunks)], axis=1)
        sv_heads.append(_dot(ws_m, v_h))
    sv = jnp.concatenate(
        [jnp.concatenate([s[:, c * CHUNK:(c + 1) * CHUNK] for s in sv_heads],
                         axis=1) + w["bsx"][...]
         for c in range(n_chunks)], axis=0)
    slot["gl"][...] = src["u"][...] * sv


def _scan_out_stage(slot, x_ref, mod_ref, w, state_scr, mix_scr, o_ref):
    tok = slot["sz"].shape[0]
    n_chunks = tok // CHUNK
    low_half = lax.broadcasted_iota(jnp.int32, (CHUNK, CHUNK), 1) < SSD_HEAD_DIM

    y_rows = []
    for c in range(n_chunks):
        r0 = c * CHUNK
        y_parts = []
        for g in range(SSD_GROUPS):
            state = state_scr[g]
            rhs_g = jnp.concatenate(
                [slot["xsb"][r0:r0 + CHUNK, g * GROUP_W:(g + 1) * GROUP_W],
                 state.astype(BF16)], axis=0)
            for j in range(HEADS_PER_GROUP // 2):
                rhs = rhs_g[:, j * LANES:(j + 1) * LANES]
                hd = c * SSD_HEADS + g * HEADS_PER_GROUP + 2 * j
                y_even = _dot(slot["lhs"][hd], rhs)
                y_odd = _dot(slot["lhs"][hd + 1], rhs)
                y_parts.append(jnp.where(low_half, y_even, y_odd))
            dec = slot["dec"][c * SSD_GROUPS + g, 0:1, :]
            state_scr[g] = state * dec + _dot(
                slot["bt"][c * SSD_GROUPS + g],
                slot["xw"][r0:r0 + CHUNK, g * GROUP_W:(g + 1) * GROUP_W])
        y_rows.append(jnp.concatenate(y_parts, axis=1))
    y_ssd = jnp.concatenate(y_rows, axis=0)

    yg_all = (y_ssd + slot["dsx"][...]) * slot["sz"][...]
    for g in range(SSD_GROUPS):
        yg = yg_all[:, g * GROUP_W:(g + 1) * GROUP_W]
        mix_scr[:, g * GROUP_W:(g + 1) * GROUP_W] = (
            (yg * _rms_scale(yg)) * w["ssdg"][:, g * GROUP_W:(g + 1) * GROUP_W]
        ).astype(BF16)
    gl = slot["gl"][...]
    mix_scr[:, D_SSD:] = ((gl * _rms_scale(gl)) * w["gog"][...]).astype(BF16)
    out = _dot(mix_scr[...], w["wo"][...])
    g1 = mod_ref[0, 0][2:3]
    o_ref[0] = x_ref[0] + g1 * out


_MIXER_CONSTS = ("n1g", "win", "wdtt", "cw", "cb", "dtb",
                 "alog", "dsk", "ssdg", "vng", "ws", "bsx", "gog", "wo")


def _mixer_kernel(*refs, tiles_per_seq):
    n_const = len(_MIXER_CONSTS)
    x_ref, x_out_ref, mod_in_ref, mod_out_ref = refs[:4]
    w = {name: ref.at[0] for name, ref in zip(_MIXER_CONSTS, refs[4:4 + n_const])}
    o_ref = refs[4 + n_const]
    xbc_scr, state_scr, hb_scr, mix_scr = refs[5 + n_const:9 + n_const]
    n_proj = 2 * len(_PROJ_FIELDS)
    proj_slots = _slot_dicts(refs[9 + n_const:9 + n_const + n_proj], _PROJ_FIELDS)
    decay_slots = _slot_dicts(refs[9 + n_const + n_proj:], _DECAY_FIELDS)
    k = pl.program_id(0)

    @pl.when(k == 0)
    def _():
        _zero_slot(proj_slots[1])
        _zero_slot(decay_slots[0])

    @pl.when(k % tiles_per_seq == 0)
    def _():
        xbc_scr[0:HIST, :] = jnp.zeros((HIST, CONV_DIM), F32)

    @pl.when(jnp.logical_or(k <= 1, (k - 2) % tiles_per_seq == 0))
    def _():
        state_scr[...] = jnp.zeros(state_scr.shape, F32)

    def step(p):
        _scan_out_stage(decay_slots[p], x_out_ref, mod_out_ref, w, state_scr,
                        mix_scr, o_ref)
        _decay_stage(proj_slots[1 - p], w, decay_slots[1 - p])
        _project_stage(x_ref, mod_in_ref, w, xbc_scr, hb_scr, proj_slots[p])

    @pl.when(k % 2 == 0)
    def _():
        step(0)

    @pl.when(k % 2 == 1)
    def _():
        step(1)


def _mixer(x, mod, p, l):
    bsz, seq, _ = x.shape
    tps = seq // TOK
    n_tiles = bsz * tps
    consts = [p[name] for name in _MIXER_CONSTS]

    def in_tile(k):
        t = jnp.minimum(k, n_tiles - 1)
        return t // tps, t % tps

    def out_tile(k):
        t = jnp.maximum(k - 2, 0)
        return t // tps, t % tps

    return pl.pallas_call(
        functools.partial(_mixer_kernel, tiles_per_seq=tps),
        grid=(n_tiles + 2,),
        in_specs=[
            pl.BlockSpec((1, TOK, D_MODEL), lambda k: (*in_tile(k), 0)),
            pl.BlockSpec((1, TOK, D_MODEL), lambda k: (*out_tile(k), 0)),
            pl.BlockSpec((1, 1, N_MOD, D_MODEL),
                         lambda k: (l, in_tile(k)[0], 0, 0)),
            pl.BlockSpec((1, 1, N_MOD, D_MODEL),
                         lambda k: (l, out_tile(k)[0], 0, 0)),
        ] + [_layer_spec(a, l) for a in consts],
        out_specs=pl.BlockSpec((1, TOK, D_MODEL), lambda k: (*out_tile(k), 0)),
        out_shape=jax.ShapeDtypeStruct(x.shape, F32),
        scratch_shapes=[
            pltpu.VMEM((HIST + TOK, CONV_DIM), F32),
            pltpu.VMEM((SSD_GROUPS, D_STATE, GROUP_W), F32),
            pltpu.VMEM((TOK, D_MODEL), BF16),
            pltpu.VMEM((TOK, D_SSD + D_GM), BF16),
        ] + 2 * _slot_shapes(_PROJ_FIELDS) + 2 * _slot_shapes(_DECAY_FIELDS),
        compiler_params=pltpu.CompilerParams(
            dimension_semantics=("arbitrary",),
            vmem_limit_bytes=VMEM_LIMIT),
        name="mixer",
    )(x, x, mod, mod, *consts)


def _ffn_up_stage(x_ref, mod_ref, w, gate_scr, hb_scr, slot):
    tok = x_ref.shape[1]
    x = x_ref[0]
    mod = mod_ref[0, 0]
    sh2, sc2 = mod[3:4], mod[4:5]
    h = (x * _rms_scale(x)) * (w["n2g"][...] * (1.0 + sc2)) + sh2
    hb_scr[...] = h.astype(BF16)
    slot["x"][...] = x
    gate_scr[HIST:HIST + tok, :] = _dot(hb_scr[...], w["wg"][...])
    val = _dot(hb_scr[...], w["wval"][...])
    acc = _causal_conv(gate_scr[0:HIST + tok, :], w["fcw"], w["fcb"], FF_CONV)
    gate_scr[0:HIST, :] = gate_scr[tok:tok + HIST, :]
    slot["act"][...] = (_silu(acc) * val).astype(BF16)


def _ffn_down_stage(slot, mod_ref, w, dst_ref):
    g2 = mod_ref[0, 0][5:6]
    dst_ref[...] = slot["x"][...] + g2 * _dot(slot["act"][...], w["wd"][...])


def _final_norm_stage(slot, w, o_ref):
    xo = slot["xo"][...]
    o_ref[0] = (xo * _rms_scale(xo)) * w["fg"][...]


_FFN_CONSTS = ("n2g", "wg", "wval", "fcw", "fcb", "wd", "fg")
_FFN_FIELDS = (("x", (FFN_TOK, D_MODEL), F32), ("act", (FFN_TOK, D_FF), BF16))
_FFN_OUT_FIELDS = (("xo", (FFN_TOK, D_MODEL), F32),)


def _ffn_kernel(*refs, tiles_per_seq, final):
    n_const = len(_FFN_CONSTS)
    x_ref, mod_in_ref, mod_down_ref = refs[:3]
    w = {name: ref.at[0] for name, ref in zip(_FFN_CONSTS, refs[3:3 + n_const])}
    o_ref = refs[3 + n_const]
    gate_scr, hb_scr = refs[4 + n_const:6 + n_const]
    n_slot = 2 * len(_FFN_FIELDS)
    slots = _slot_dicts(refs[6 + n_const:6 + n_const + n_slot], _FFN_FIELDS)
    out_slots = _slot_dicts(refs[6 + n_const + n_slot:], _FFN_OUT_FIELDS)
    k = pl.program_id(0)

    @pl.when(k == 0)
    def _():
        _zero_slot(slots[1])
        if final:
            _zero_slot(out_slots[0])

    @pl.when(k % tiles_per_seq == 0)
    def _():
        gate_scr[0:HIST, :] = jnp.zeros((HIST, D_FF), F32)

    def step(p):
        if final:
            _final_norm_stage(out_slots[p], w, o_ref)
            _ffn_down_stage(slots[1 - p], mod_down_ref, w, out_slots[1 - p]["xo"])
        else:
            _ffn_down_stage(slots[1 - p], mod_down_ref, w, o_ref.at[0])
        _ffn_up_stage(x_ref, mod_in_ref, w, gate_scr, hb_scr, slots[p])

    @pl.when(k % 2 == 0)
    def _():
        step(0)

    @pl.when(k % 2 == 1)
    def _():
        step(1)


def _ffn(x, mod, p, l, final):
    bsz, seq, _ = x.shape
    tps = seq // FFN_TOK
    n_tiles = bsz * tps
    lag = 2 if final else 1
    half_up = (D_MODEL, D_FF)
    operands = [(p["n2g"], _layer_spec(p["n2g"], l)),
                (p["wup"], _layer_spec(p["wup"], l, half_up, (0, 0))),
                (p["wup"], _layer_spec(p["wup"], l, half_up, (0, 1))),
                (p["fcw"], _layer_spec(p["fcw"], l)),
                (p["fcb"], _layer_spec(p["fcb"], l)),
                (p["wd"], _layer_spec(p["wd"], l)),
                (p["fg"], _layer_spec(p["fg"], 0))]

    def tile(k, behind):
        t = jnp.clip(k - behind, 0, n_tiles - 1)
        return t // tps, t % tps

    return pl.pallas_call(
        functools.partial(_ffn_kernel, tiles_per_seq=tps, final=final),
        grid=(n_tiles + lag,),
        in_specs=[
            pl.BlockSpec((1, FFN_TOK, D_MODEL), lambda k: (*tile(k, 0), 0)),
            pl.BlockSpec((1, 1, N_MOD, D_MODEL),
                         lambda k: (l, tile(k, 0)[0], 0, 0)),
            pl.BlockSpec((1, 1, N_MOD, D_MODEL),
                         lambda k: (l, tile(k, 1)[0], 0, 0)),
        ] + [spec for _, spec in operands],
        out_specs=pl.BlockSpec((1, FFN_TOK, D_MODEL),
                               lambda k: (*tile(k, lag), 0)),
        out_shape=jax.ShapeDtypeStruct(x.shape, F32),
        scratch_shapes=[pltpu.VMEM((HIST + FFN_TOK, D_FF), F32),
                        pltpu.VMEM((FFN_TOK, D_MODEL), BF16)]
        + 2 * _slot_shapes(_FFN_FIELDS)
        + (2 * _slot_shapes(_FFN_OUT_FIELDS) if final else []),
        compiler_params=pltpu.CompilerParams(
            dimension_semantics=("arbitrary",),
            vmem_limit_bytes=VMEM_LIMIT),
        name="ffn_final" if final else "ffn",
    )(x, mod, mod, *[a for a, _ in operands])


def _prep_params(norm1_g, norm2_g, w_in, ssd_conv_w, ssd_conv_b, ssd_dt_bias,
                 ssd_a_log, ssd_d, ssd_norm_g, gm_vnorm_g, gm_ws, gm_bs,
                 gm_out_g, w_out, ff_up, ff_conv_w, ff_conv_b, ff_down, final_g):
    o_dt = D_SSD + CONV_DIM
    w_in_t = jnp.swapaxes(w_in, 1, 2)
    row = lambda a: a[:, None, :]
    col = lambda a: a[:, :, None]
    return {
        "n1g": row(norm1_g),
        "win": _regroup_win(w_in_t),
        "wdtt": w_in_t[:, o_dt:o_dt + SSD_HEADS, :].astype(BF16),
        "cw": ssd_conv_w,
        "cb": row(ssd_conv_b),
        "dtb": col(ssd_dt_bias),
        "alog": col(ssd_a_log),
        "dsk": row(jnp.repeat(ssd_d, SSD_HEAD_DIM, axis=1)),
        "ssdg": row(ssd_norm_g),
        "vng": row(gm_vnorm_g),
        "ws": gm_ws,
        "bsx": jnp.repeat(jnp.swapaxes(gm_bs, 1, 2), GM_HEAD_DIM, axis=2),
        "gog": row(gm_out_g),
        "wo": w_out.astype(BF16),
        "n2g": row(norm2_g),
        "wup": ff_up.astype(BF16),
        "fcw": ff_conv_w,
        "fcb": row(ff_conv_b),
        "wd": ff_down.astype(BF16),
        "fg": final_g.reshape(1, 1, D_MODEL),
    }


def kernel(x, c, ada_w, ada_b, norm1_g, norm2_g, w_in, ssd_conv_w, ssd_conv_b, ssd_dt_bias, ssd_a_log, ssd_d, ssd_norm_g, gm_vnorm_g, gm_ws, gm_bs, gm_out_g, w_out, ff_up, ff_conv_w, ff_conv_b, ff_down, final_g):
    depth = ada_w.shape[0]
    bsz = x.shape[0]
    mod = _ada(c, ada_w, ada_b).reshape(depth, bsz, N_MOD, D_MODEL)
    p = _prep_params(norm1_g, norm2_g, w_in, ssd_conv_w, ssd_conv_b, ssd_dt_bias,
                     ssd_a_log, ssd_d, ssd_norm_g, gm_vnorm_g, gm_ws, gm_bs,
                     gm_out_g, w_out, ff_up, ff_conv_w, ff_conv_b, ff_down,
                     final_g)
    for l in range(depth):
        x = _mixer(x, mod, p, l)
        x = _ffn(x, mod, p, l, final=(l == depth - 1))
    return x
```

```python
import functools

import jax
import jax.numpy as jnp
from jax import lax
from jax.experimental import pallas as pl
from jax.experimental.pallas import tpu as pltpu

F32 = jnp.float32
BF16 = jnp.bfloat16

D_MODEL = 1024
SSD_HEAD_DIM = 64
SSD_HEADS = 16
D_SSD = 1024
SSD_GROUPS = 2
HEADS_PER_GROUP = 8
D_STATE = 128
SSD_CONV = 4
CHUNK = 128
CONV_DIM = D_SSD + 2 * SSD_GROUPS * D_STATE
GM_HEAD_DIM = 128
GM_HEADS = 8
D_GM = 1024
D_FF = 2816
FF_CONV = 3
N_MOD = 6
EPS = 1e-6
LOG2E = 1.4426950408889634

LANES = 128
SUBLANES = 8
HIST = SUBLANES
TOK = 256
FFN_TOK = 512
N_CHUNKS = TOK // CHUNK
VMEM_LIMIT = 56 * 1024 * 1024
GROUP_W = HEADS_PER_GROUP * SSD_HEAD_DIM


def _dot(a, b):
    return jnp.dot(a, b, preferred_element_type=F32)


def _dot_nt(a, b):
    return lax.dot_general(a, b, (((1,), (1,)), ((), ())),
                           preferred_element_type=F32)


def _silu(x):
    half = 0.5 * x
    return half + half * jnp.tanh(half)


def _gelu(x):
    return 0.5 * x * (1.0 + lax.erf(x * 0.7071067811865476))


def _rms_scale(x):
    return lax.rsqrt(jnp.mean(x * x, axis=-1, keepdims=True) + EPS)


def _split3(x):
    hi = x.astype(BF16)
    r1 = x - hi.astype(F32)
    mid = r1.astype(BF16)
    lo = (r1 - mid.astype(F32)).astype(BF16)
    return hi, mid, lo


def _causal_conv(full, w_ref, b_ref, width):
    acc = b_ref[...] + w_ref[width - 1:width, :] * full[HIST:]
    for j in range(1, width):
        acc = acc + w_ref[width - 1 - j:width - j, :] * pltpu.roll(full, j, 0)[HIST:]
    return acc


def _slot_shapes(fields):
    return [pltpu.VMEM(shape, dt) for _, shape, dt in fields]


def _slot_dicts(refs, fields):
    n = len(fields)
    return [dict(zip([f[0] for f in fields], refs[i * n:(i + 1) * n]))
            for i in range(2)]


def _zero_slot(slot):
    for ref in slot.values():
        ref[...] = jnp.zeros(ref.shape, ref.dtype)


def _layer_spec(stacked, l, block=None, index=None):
    block = (1,) + (stacked.shape[1:] if block is None else block)
    index = (l,) + ((0,) * (len(block) - 1) if index is None else index)
    return pl.BlockSpec(block, lambda *_: index, pipeline_mode=pl.Buffered(1))


def _ada_kernel(c_ref, w_ref, b_ref, o_ref):
    ca = _silu(c_ref[...]).astype(BF16)
    o_ref[0] = _dot(ca, w_ref[0].astype(BF16)) + b_ref[0]


def _ada(c, ada_w, ada_b):
    n_layers, _, n_out = ada_w.shape
    bsz = c.shape[0]
    tn = 1536
    return pl.pallas_call(
        _ada_kernel,
        grid=(n_layers, n_out // tn),
        in_specs=[
            pl.BlockSpec((bsz, D_MODEL), lambda l, j: (0, 0)),
            pl.BlockSpec((1, D_MODEL, tn), lambda l, j: (l, 0, j)),
            pl.BlockSpec((1, 1, tn), lambda l, j: (l, 0, j)),
        ],
        out_specs=pl.BlockSpec((1, bsz, tn), lambda l, j: (l, 0, j)),
        out_shape=jax.ShapeDtypeStruct((n_layers, bsz, n_out), F32),
        compiler_params=pltpu.CompilerParams(
            dimension_semantics=("arbitrary", "arbitrary"),
            vmem_limit_bytes=VMEM_LIMIT),
        name="ada_mod",
    )(c, ada_w, ada_b.reshape(n_layers, 1, n_out))


PREP_COLS = 512
N_ZX_BLOCKS = (D_SSD + CONV_DIM) // PREP_COLS


def _regroup_kernel(a_ref, b_ref, o_ref):
    j = pl.program_id(1)
    a = a_ref[0]
    past_gap = jnp.concatenate([a[SSD_HEADS:], b_ref[0]], axis=0)
    src = jnp.where(j < N_ZX_BLOCKS, a, past_gap)
    o_ref[0] = src.T.astype(BF16)


def _regroup_win(w_in_t):
    n_layers, n_in, k_dim = w_in_t.shape
    n_out = n_in - SSD_HEADS
    return pl.pallas_call(
        _regroup_kernel,
        grid=(n_layers, n_out // PREP_COLS),
        in_specs=[
            pl.BlockSpec((1, PREP_COLS, k_dim), lambda l, j: (l, j, 0)),
            pl.BlockSpec((1, SSD_HEADS, k_dim),
                         lambda l, j: (l, (j + 1) * (PREP_COLS // SSD_HEADS), 0)),
        ],
        out_specs=pl.BlockSpec((1, k_dim, PREP_COLS), lambda l, j: (l, 0, j)),
        out_shape=jax.ShapeDtypeStruct((n_layers, k_dim, n_out), BF16),
        compiler_params=pltpu.CompilerParams(
            dimension_semantics=("arbitrary", "arbitrary"),
            vmem_limit_bytes=VMEM_LIMIT),
        name="win_prep",
    )(w_in_t, w_in_t)


_PROJ_FIELDS = (
    ("z", (TOK, D_SSD), F32),
    ("xs", (TOK, D_SSD), F32),
    ("bc", (TOK, 2 * SSD_GROUPS * D_STATE), F32),
    ("dt", (SSD_HEADS, TOK), F32),
    ("u", (TOK, D_GM), F32),
    ("vn", (TOK, D_GM), BF16),
)
_DECAY_FIELDS = (
    ("sz", (TOK, D_SSD), F32),
    ("dsx", (TOK, D_SSD), F32),
    ("xsb", (TOK, D_SSD), BF16),
    ("gl", (TOK, D_GM), F32),
    ("lhs", (N_CHUNKS * SSD_HEADS, CHUNK, CHUNK + D_STATE), BF16),
    ("xw", (TOK, D_SSD), BF16),
    ("bt", (N_CHUNKS * SSD_GROUPS, D_STATE, CHUNK), BF16),
    ("dec", (N_CHUNKS * SSD_GROUPS, SUBLANES, GROUP_W), F32),
)


def _project_stage(x_ref, mod_ref, w, xbc_scr, hb_scr, slot):
    tok = x_ref.shape[1]
    x = x_ref[0]
    mod = mod_ref[0, 0]
    sh1, sc1 = mod[0:1], mod[1:2]
    h = (x * _rms_scale(x)) * (w["n1g"][...] * (1.0 + sc1)) + sh1
    hb_scr[...] = h.astype(BF16)
    proj = _dot(hb_scr[...], w["win"][...])
    slot["z"][...] = proj[:, :D_SSD]
    xbc_scr[HIST:HIST + tok, :] = proj[:, D_SSD:D_SSD + CONV_DIM]
    u_raw = proj[:, D_SSD + CONV_DIM:D_SSD + CONV_DIM + D_GM]
    v_raw = proj[:, D_SSD + CONV_DIM + D_GM:]
    slot["dt"][...] = _dot_nt(w["wdtt"][...], hb_scr[...])
    acc = _causal_conv(xbc_scr[0:HIST + tok, :], w["cw"], w["cb"], SSD_CONV)
    xbc_scr[0:HIST, :] = xbc_scr[tok:tok + HIST, :]
    xbc = _silu(acc)
    slot["xs"][...] = xbc[:, :D_SSD]
    slot["bc"][...] = xbc[:, D_SSD:]
    slot["u"][...] = _gelu(u_raw)
    v = _gelu(v_raw)
    slot["vn"][...] = ((v * _rms_scale(v)) * w["vng"][...]).astype(BF16)


def _decay_stage(src, w, slot):
    tok = src["xs"].shape[0]
    n_chunks = tok // CHUNK
    xs = src["xs"][...]
    bc = src["bc"][...]
    bmat = bc[:, :SSD_GROUPS * D_STATE]
    cmat = bc[:, SSD_GROUPS * D_STATE:]
    slot["sz"][...] = _silu(src["z"][...])
    slot["xsb"][...] = xs.astype(BF16)
    slot["dsx"][...] = w["dsk"][...] * xs

    dt_in = src["dt"][...] + w["dtb"][...]
    dt_t = jnp.maximum(dt_in, 0.0) + jnp.log1p(jnp.exp(-jnp.abs(dt_in)))
    dta_t = dt_t * (-jnp.exp(w["alog"][...]))
    log2_dt_t = jnp.log(dt_t) * LOG2E

    li = lax.broadcasted_iota(jnp.int32, (CHUNK, CHUNK), 0)
    si = lax.broadcasted_iota(jnp.int32, (CHUNK, CHUNK), 1)
    causal = li >= si
    triu_b = jnp.where(li <= si, 1.0, 0.0).astype(BF16)
    low_half = si[0:1] < SSD_HEAD_DIM

    for c in range(tok // CHUNK):
        r0 = c * CHUNK
        hi, mid, lo = _split3(dta_t[:, r0:r0 + CHUNK])
        a2_t = (_dot(hi, triu_b) + _dot(mid, triu_b) + _dot(lo, triu_b)) * LOG2E
        dt_c = dt_t[:, r0:r0 + CHUNK]
        row_t = a2_t - log2_dt_t[:, r0:r0 + CHUNK]
        w_t = dt_c * jnp.exp2(a2_t[:, CHUNK - 1:CHUNK] - a2_t)
        for g in range(SSD_GROUPS):
            b_blk = bmat[r0:r0 + CHUNK, g * D_STATE:(g + 1) * D_STATE]
            c_blk = cmat[r0:r0 + CHUNK, g * D_STATE:(g + 1) * D_STATE]
            c_b = c_blk.astype(BF16)
            cb_b = _dot_nt(c_b, b_blk.astype(BF16)).astype(BF16)
            slot["bt"][c * SSD_GROUPS + g] = b_blk.T.astype(BF16)
            w_parts, dec_parts = [], []
            for j in range(HEADS_PER_GROUP // 2):
                w_pair, e_pair = [], []
                for hd in (g * HEADS_PER_GROUP + 2 * j,
                           g * HEADS_PER_GROUP + 2 * j + 1):
                    col = jnp.broadcast_to(a2_t[hd:hd + 1, :], (CHUNK, CHUNK)).T
                    seg = jnp.where(causal, col - row_t[hd:hd + 1, :], -jnp.inf)
                    e_col = jnp.exp2(col)
                    slot["lhs"][c * SSD_HEADS + hd] = jnp.concatenate(
                        [cb_b * jnp.exp2(seg).astype(BF16),
                         c_b * e_col.astype(BF16)], axis=1)
                    w_pair.append(jnp.broadcast_to(
                        w_t[hd:hd + 1, :], (SSD_HEAD_DIM, CHUNK)))
                    e_pair.append(e_col[CHUNK - 1:CHUNK, :])
                w_parts.append(jnp.concatenate(w_pair, axis=0).T)
                dec_parts.append(jnp.where(low_half, e_pair[0], e_pair[1]))
            w_exp = jnp.concatenate(w_parts, axis=1)
            slot["xw"][r0:r0 + CHUNK, g * GROUP_W:(g + 1) * GROUP_W] = (
                xs[r0:r0 + CHUNK, g * GROUP_W:(g + 1) * GROUP_W] * w_exp
            ).astype(BF16)
            slot["dec"][c * SSD_GROUPS + g, 0:1, :] = jnp.concatenate(
                dec_parts, axis=1)

    vn = src["vn"][...]
    sv_heads = []
    for hd in range(GM_HEADS):
        ws_m = jnp.where(causal, w["ws"][hd], 0.0).astype(BF16)
        v_h = jnp.concatenate(
            [vn[c * CHUNK:(c + 1) * CHUNK,
                hd * GM_HEAD_DIM:(hd + 1) * GM_HEAD_DIM]
             for c in range(n_chunks)], axis=1)
        sv_heads.append(_dot(ws_m, v_h))
    sv = jnp.concatenate(
        [jnp.concatenate([s[:, c * CHUNK:(c + 1) * CHUNK] for s in sv_heads],
                         axis=1) + w["bsx"][...]
         for c in range(n_chunks)], axis=0)
    slot["gl"][...] = src["u"][...] * sv


def _scan_out_stage(slot, x_ref, mod_ref, w, state_scr, mix_scr, o_ref):
    tok = slot["sz"].shape[0]
    n_chunks = tok // CHUNK
    low_half = lax.broadcasted_iota(jnp.int32, (CHUNK, CHUNK), 1) < SSD_HEAD_DIM

    y_rows = []
    for c in range(n_chunks):
        r0 = c * CHUNK
        y_parts = []
        for g in range(SSD_GROUPS):
            state = state_scr[g]
            rhs_g = jnp.concatenate(
                [slot["xsb"][r0:r0 + CHUNK, g * GROUP_W:(g + 1) * GROUP_W],
                 state.astype(BF16)], axis=0)
            for j in range(HEADS_PER_GROUP // 2):
                rhs = rhs_g[:, j * LANES:(j + 1) * LANES]
                hd = c * SSD_HEADS + g * HEADS_PER_GROUP + 2 * j
                y_even = _dot(slot["lhs"][hd], rhs)
                y_odd = _dot(slot["lhs"][hd + 1], rhs)
                y_parts.append(jnp.where(low_half, y_even, y_odd))
            dec = slot["dec"][c * SSD_GROUPS + g, 0:1, :]
            state_scr[g] = state * dec + _dot(
                slot["bt"][c * SSD_GROUPS + g],
                slot["xw"][r0:r0 + CHUNK, g * GROUP_W:(g + 1) * GROUP_W])
        y_rows.append(jnp.concatenate(y_parts, axis=1))
    y_ssd = jnp.concatenate(y_rows, axis=0)

    yg_all = (y_ssd + slot["dsx"][...]) * slot["sz"][...]
    for g in range(SSD_GROUPS):
        yg = yg_all[:, g * GROUP_W:(g + 1) * GROUP_W]
        mix_scr[:, g * GROUP_W:(g + 1) * GROUP_W] = (
            (yg * _rms_scale(yg)) * w["ssdg"][:, g * GROUP_W:(g + 1) * GROUP_W]
        ).astype(BF16)
    gl = slot["gl"][...]
    mix_scr[:, D_SSD:] = ((gl * _rms_scale(gl)) * w["gog"][...]).astype(BF16)
    out = _dot(mix_scr[...], w["wo"][...])
    g1 = mod_ref[0, 0][2:3]
    o_ref[0] = x_ref[0] + g1 * out


_MIXER_CONSTS = ("n1g", "win", "wdtt", "cw", "cb", "dtb",
                 "alog", "dsk", "ssdg", "vng", "ws", "bsx", "gog", "wo")


def _mixer_kernel(*refs, tiles_per_seq):
    n_const = len(_MIXER_CONSTS)
    x_ref, x_out_ref, mod_in_ref, mod_out_ref = refs[:4]
    w = {name: ref.at[0] for name, ref in zip(_MIXER_CONSTS, refs[4:4 + n_const])}
    o_ref = refs[4 + n_const]
    xbc_scr, state_scr, hb_scr, mix_scr = refs[5 + n_const:9 + n_const]
    n_proj = 2 * len(_PROJ_FIELDS)
    proj_slots = _slot_dicts(refs[9 + n_const:9 + n_const + n_proj], _PROJ_FIELDS)
    decay_slots = _slot_dicts(refs[9 + n_const + n_proj:], _DECAY_FIELDS)
    k = pl.program_id(0)

    @pl.when(k == 0)
    def _():
        _zero_slot(proj_slots[1])
        _zero_slot(decay_slots[0])

    @pl.when(k % tiles_per_seq == 0)
    def _():
        xbc_scr[0:HIST, :] = jnp.zeros((HIST, CONV_DIM), F32)

    @pl.when(jnp.logical_or(k <= 1, (k - 2) % tiles_per_seq == 0))
    def _():
        state_scr[...] = jnp.zeros(state_scr.shape, F32)

    def step(p):
        _scan_out_stage(decay_slots[p], x_out_ref, mod_out_ref, w, state_scr,
                        mix_scr, o_ref)
        _decay_stage(proj_slots[1 - p], w, decay_slots[1 - p])
        _project_stage(x_ref, mod_in_ref, w, xbc_scr, hb_scr, proj_slots[p])

    @pl.when(k % 2 == 0)
    def _():
        step(0)

    @pl.when(k % 2 == 1)
    def _():
        step(1)


def _mixer(x, mod, p, l):
    bsz, seq, _ = x.shape
    tps = seq // TOK
    n_tiles = bsz * tps
    consts = [p[name] for name in _MIXER_CONSTS]

    def in_tile(k):
        t = jnp.minimum(k, n_tiles - 1)
        return t // tps, t % tps

    def out_tile(k):
        t = jnp.maximum(k - 2, 0)
        return t // tps, t % tps

    return pl.pallas_call(
        functools.partial(_mixer_kernel, tiles_per_seq=tps),
        grid=(n_tiles + 2,),
        in_specs=[
            pl.BlockSpec((1, TOK, D_MODEL), lambda k: (*in_tile(k), 0)),
            pl.BlockSpec((1, TOK, D_MODEL), lambda k: (*out_tile(k), 0)),
            pl.BlockSpec((1, 1, N_MOD, D_MODEL),
                         lambda k: (l, in_tile(k)[0], 0, 0)),
            pl.BlockSpec((1, 1, N_MOD, D_MODEL),
                         lambda k: (l, out_tile(k)[0], 0, 0)),
        ] + [_layer_spec(a, l) for a in consts],
        out_specs=pl.BlockSpec((1, TOK, D_MODEL), lambda k: (*out_tile(k), 0)),
        out_shape=jax.ShapeDtypeStruct(x.shape, F32),
        scratch_shapes=[
            pltpu.VMEM((HIST + TOK, CONV_DIM), F32),
            pltpu.VMEM((SSD_GROUPS, D_STATE, GROUP_W), F32),
            pltpu.VMEM((TOK, D_MODEL), BF16),
            pltpu.VMEM((TOK, D_SSD + D_GM), BF16),
        ] + 2 * _slot_shapes(_PROJ_FIELDS) + 2 * _slot_shapes(_DECAY_FIELDS),
        compiler_params=pltpu.CompilerParams(
            dimension_semantics=("arbitrary",),
            vmem_limit_bytes=VMEM_LIMIT),
        name="mixer",
    )(x, x, mod, mod, *consts)


def _ffn_up_stage(x_ref, mod_ref, w, gate_scr, hb_scr, slot):
    tok = x_ref.shape[1]
    x = x_ref[0]
    mod = mod_ref[0, 0]
    sh2, sc2 = mod[3:4], mod[4:5]
    h = (x * _rms_scale(x)) * (w["n2g"][...] * (1.0 + sc2)) + sh2
    hb_scr[...] = h.astype(BF16)
    slot["x"][...] = x
    gate_scr[HIST:HIST + tok, :] = _dot(hb_scr[...], w["wg"][...])
    val = _dot(hb_scr[...], w["wval"][...])
    acc = _causal_conv(gate_scr[0:HIST + tok, :], w["fcw"], w["fcb"], FF_CONV)
    gate_scr[0:HIST, :] = gate_scr[tok:tok + HIST, :]
    slot["act"][...] = (_silu(acc) * val).astype(BF16)


def _ffn_down_stage(slot, mod_ref, w, dst_ref):
    g2 = mod_ref[0, 0][5:6]
    dst_ref[...] = slot["x"][...] + g2 * _dot(slot["act"][...], w["wd"][...])


def _final_norm_stage(slot, w, o_ref):
    xo = slot["xo"][...]
    o_ref[0] = (xo * _rms_scale(xo)) * w["fg"][...]


_FFN_CONSTS = ("n2g", "wg", "wval", "fcw", "fcb", "wd", "fg")
_FFN_FIELDS = (("x", (FFN_TOK, D_MODEL), F32), ("act", (FFN_TOK, D_FF), BF16))
_FFN_OUT_FIELDS = (("xo", (FFN_TOK, D_MODEL), F32),)


def _ffn_kernel(*refs, tiles_per_seq, final):
    n_const = len(_FFN_CONSTS)
    x_ref, mod_in_ref, mod_down_ref = refs[:3]
    w = {name: ref.at[0] for name, ref in zip(_FFN_CONSTS, refs[3:3 + n_const])}
    o_ref = refs[3 + n_const]
    gate_scr, hb_scr = refs[4 + n_const:6 + n_const]
    n_slot = 2 * len(_FFN_FIELDS)
    slots = _slot_dicts(refs[6 + n_const:6 + n_const + n_slot], _FFN_FIELDS)
    out_slots = _slot_dicts(refs[6 + n_const + n_slot:], _FFN_OUT_FIELDS)
    k = pl.program_id(0)

    @pl.when(k == 0)
    def _():
        _zero_slot(slots[1])
        if final:
            _zero_slot(out_slots[0])

    @pl.when(k % tiles_per_seq == 0)
    def _():
        gate_scr[0:HIST, :] = jnp.zeros((HIST, D_FF), F32)

    def step(p):
        if final:
            _final_norm_stage(out_slots[p], w, o_ref)
            _ffn_down_stage(slots[1 - p], mod_down_ref, w, out_slots[1 - p]["xo"])
        else:
            _ffn_down_stage(slots[1 - p], mod_down_ref, w, o_ref.at[0])
        _ffn_up_stage(x_ref, mod_in_ref, w, gate_scr, hb_scr, slots[p])

    @pl.when(k % 2 == 0)
    def _():
        step(0)

    @pl.when(k % 2 == 1)
    def _():
        step(1)


def _ffn(x, mod, p, l, final):
    bsz, seq, _ = x.shape
    tps = seq // FFN_TOK
    n_tiles = bsz * tps
    lag = 2 if final else 1
    half_up = (D_MODEL, D_FF)
    operands = [(p["n2g"], _layer_spec(p["n2g"], l)),
                (p["wup"], _layer_spec(p["wup"], l, half_up, (0, 0))),
                (p["wup"], _layer_spec(p["wup"], l, half_up, (0, 1))),
                (p["fcw"], _layer_spec(p["fcw"], l)),
                (p["fcb"], _layer_spec(p["fcb"], l)),
                (p["wd"], _layer_spec(p["wd"], l)),
                (p["fg"], _layer_spec(p["fg"], 0))]

    def tile(k, behind):
        t = jnp.clip(k - behind, 0, n_tiles - 1)
        return t // tps, t % tps

    return pl.pallas_call(
        functools.partial(_ffn_kernel, tiles_per_seq=tps, final=final),
        grid=(n_tiles + lag,),
        in_specs=[
            pl.BlockSpec((1, FFN_TOK, D_MODEL), lambda k: (*tile(k, 0), 0)),
            pl.BlockSpec((1, 1, N_MOD, D_MODEL),
                         lambda k: (l, tile(k, 0)[0], 0, 0)),
            pl.BlockSpec((1, 1, N_MOD, D_MODEL),
                         lambda k: (l, tile(k, 1)[0], 0, 0)),
        ] + [spec for _, spec in operands],
        out_specs=pl.BlockSpec((1, FFN_TOK, D_MODEL),
                               lambda k: (*tile(k, lag), 0)),
        out_shape=jax.ShapeDtypeStruct(x.shape, F32),
        scratch_shapes=[pltpu.VMEM((HIST + FFN_TOK, D_FF), F32),
                        pltpu.VMEM((FFN_TOK, D_MODEL), BF16)]
        + 2 * _slot_shapes(_FFN_FIELDS)
        + (2 * _slot_shapes(_FFN_OUT_FIELDS) if final else []),
        compiler_params=pltpu.CompilerParams(
            dimension_semantics=("arbitrary",),
            vmem_limit_bytes=VMEM_LIMIT),
        name="ffn_final" if final else "ffn",
    )(x, mod, mod, *[a for a, _ in operands])


def _prep_params(norm1_g, norm2_g, w_in, ssd_conv_w, ssd_conv_b, ssd_dt_bias,
                 ssd_a_log, ssd_d, ssd_norm_g, gm_vnorm_g, gm_ws, gm_bs,
                 gm_out_g, w_out, ff_up, ff_conv_w, ff_conv_b, ff_down, final_g):
    o_dt = D_SSD + CONV_DIM
    w_in_t = jnp.swapaxes(w_in, 1, 2)
    row = lambda a: a[:, None, :]
    col = lambda a: a[:, :, None]
    return {
        "n1g": row(norm1_g),
        "win": _regroup_win(w_in_t),
        "wdtt": w_in_t[:, o_dt:o_dt + SSD_HEADS, :].astype(BF16),
        "cw": ssd_conv_w,
        "cb": row(ssd_conv_b),
        "dtb": col(ssd_dt_bias),
        "alog": col(ssd_a_log),
        "dsk": row(jnp.repeat(ssd_d, SSD_HEAD_DIM, axis=1)),
        "ssdg": row(ssd_norm_g),
        "vng": row(gm_vnorm_g),
        "ws": gm_ws,
        "bsx": jnp.repeat(jnp.swapaxes(gm_bs, 1, 2), GM_HEAD_DIM, axis=2),
        "gog": row(gm_out_g),
        "wo": w_out.astype(BF16),
        "n2g": row(norm2_g),
        "wup": ff_up.astype(BF16),
        "fcw": ff_conv_w,
        "fcb": row(ff_conv_b),
        "wd": ff_down.astype(BF16),
        "fg": final_g.reshape(1, 1, D_MODEL),
    }


def kernel(x, c, ada_w, ada_b, norm1_g, norm2_g, w_in, ssd_conv_w, ssd_conv_b, ssd_dt_bias, ssd_a_log, ssd_d, ssd_norm_g, gm_vnorm_g, gm_ws, gm_bs, gm_out_g, w_out, ff_up, ff_conv_w, ff_conv_b, ff_down, final_g):
    depth = ada_w.shape[0]
    bsz = x.shape[0]
    mod = _ada(c, ada_w, ada_b).reshape(depth, bsz, N_MOD, D_MODEL)
    p = _prep_params(norm1_g, norm2_g, w_in, ssd_conv_w, ssd_conv_b, ssd_dt_bias,
                     ssd_a_log, ssd_d, ssd_norm_g, gm_vnorm_g, gm_ws, gm_bs,
                     gm_out_g, w_out, ff_up, ff_conv_w, ff_conv_b, ff_down,
                     final_g)
    for l in range(depth):
        x = _mixer(x, mod, p, l)
        x = _ffn(x, mod, p, l, final=(l == depth - 1))
    return x
```

```python
import functools

import jax
import jax.numpy as jnp
from jax import lax
from jax.experimental import pallas as pl
from jax.experimental.pallas import tpu as pltpu

F32 = jnp.float32
BF16 = jnp.bfloat16

D_MODEL = 1024
SSD_HEAD_DIM = 64
SSD_HEADS = 16
D_SSD = 1024
SSD_GROUPS = 2
HEADS_PER_GROUP = 8
D_STATE = 128
SSD_CONV = 4
CHUNK = 128
CONV_DIM = D_SSD + 2 * SSD_GROUPS * D_STATE
GM_HEAD_DIM = 128
GM_HEADS = 8
D_GM = 1024
D_FF = 2816
FF_CONV = 3
N_MOD = 6
EPS = 1e-6
LOG2E = 1.4426950408889634

LANES = 128
SUBLANES = 8
HIST = SUBLANES
TOK = 256
FFN_TOK = 512
N_CHUNKS = TOK // CHUNK
VMEM_LIMIT = 56 * 1024 * 1024
GROUP_W = HEADS_PER_GROUP * SSD_HEAD_DIM


def _dot(a, b):
    return jnp.dot(a, b, preferred_element_type=F32)


def _dot_nt(a, b):
    return lax.dot_general(a, b, (((1,), (1,)), ((), ())),
                           preferred_element_type=F32)


def _silu(x):
    half = 0.5 * x
    return half + half * jnp.tanh(half)


def _gelu(x):
    return 0.5 * x * (1.0 + lax.erf(x * 0.7071067811865476))


def _rms_scale(x):
    return lax.rsqrt(jnp.mean(x * x, axis=-1, keepdims=True) + EPS)


def _split3(x):
    hi = x.astype(BF16)
    r1 = x - hi.astype(F32)
    mid = r1.astype(BF16)
    lo = (r1 - mid.astype(F32)).astype(BF16)
    return hi, mid, lo


def _causal_conv(full, w_ref, b_ref, width):
    acc = b_ref[...] + w_ref[width - 1:width, :] * full[HIST:]
    for j in range(1, width):
        acc = acc + w_ref[width - 1 - j:width - j, :] * pltpu.roll(full, j, 0)[HIST:]
    return acc


def _slot_shapes(fields):
    return [pltpu.VMEM(shape, dt) for _, shape, dt in fields]


def _slot_dicts(refs, fields):
    n = len(fields)
    return [dict(zip([f[0] for f in fields], refs[i * n:(i + 1) * n]))
            for i in range(2)]


def _zero_slot(slot):
    for ref in slot.values():
        ref[...] = jnp.zeros(ref.shape, ref.dtype)


def _layer_spec(stacked, l, block=None, index=None):
    block = (1,) + (stacked.shape[1:] if block is None else block)
    index = (l,) + ((0,) * (len(block) - 1) if index is None else index)
    return pl.BlockSpec(block, lambda *_: index, pipeline_mode=pl.Buffered(1))


def _ada_kernel(c_ref, w_ref, b_ref, o_ref):
    ca = _silu(c_ref[...]).astype(BF16)
    o_ref[0] = _dot(ca, w_ref[0].astype(BF16)) + b_ref[0]


def _ada(c, ada_w, ada_b):
    n_layers, _, n_out = ada_w.shape
    bsz = c.shape[0]
    tn = 1536
    return pl.pallas_call(
        _ada_kernel,
        grid=(n_layers, n_out // tn),
        in_specs=[
            pl.BlockSpec((bsz, D_MODEL), lambda l, j: (0, 0)),
            pl.BlockSpec((1, D_MODEL, tn), lambda l, j: (l, 0, j)),
            pl.BlockSpec((1, 1, tn), lambda l, j: (l, 0, j)),
        ],
        out_specs=pl.BlockSpec((1, bsz, tn), lambda l, j: (l, 0, j)),
        out_shape=jax.ShapeDtypeStruct((n_layers, bsz, n_out), F32),
        compiler_params=pltpu.CompilerParams(
            dimension_semantics=("arbitrary", "arbitrary"),
            vmem_limit_bytes=VMEM_LIMIT),
        name="ada_mod",
    )(c, ada_w, ada_b.reshape(n_layers, 1, n_out))


PREP_COLS = 512
N_ZX_BLOCKS = (D_SSD + CONV_DIM) // PREP_COLS


def _regroup_kernel(a_ref, b_ref, o_ref):
    j = pl.program_id(1)
    a = a_ref[0]
    past_gap = jnp.concatenate([a[SSD_HEADS:], b_ref[0]], axis=0)
    src = jnp.where(j < N_ZX_BLOCKS, a, past_gap)
    o_ref[0] = src.T.astype(BF16)


def _regroup_win(w_in_t):
    n_layers, n_in, k_dim = w_in_t.shape
    n_out = n_in - SSD_HEADS
    return pl.pallas_call(
        _regroup_kernel,
        grid=(n_layers, n_out // PREP_COLS),
        in_specs=[
            pl.BlockSpec((1, PREP_COLS, k_dim), lambda l, j: (l, j, 0)),
            pl.BlockSpec((1, SSD_HEADS, k_dim),
                         lambda l, j: (l, (j + 1) * (PREP_COLS // SSD_HEADS), 0)),
        ],
        out_specs=pl.BlockSpec((1, k_dim, PREP_COLS), lambda l, j: (l, 0, j)),
        out_shape=jax.ShapeDtypeStruct((n_layers, k_dim, n_out), BF16),
        compiler_params=pltpu.CompilerParams(
            dimension_semantics=("arbitrary", "arbitrary"),
            vmem_limit_bytes=VMEM_LIMIT),
        name="win_prep",
    )(w_in_t, w_in_t)


_PROJ_FIELDS = (
    ("z", (TOK, D_SSD), F32),
    ("xs", (TOK, D_SSD), F32),
    ("bc", (TOK, 2 * SSD_GROUPS * D_STATE), F32),
    ("dt", (SSD_HEADS, TOK), F32),
    ("u", (TOK, D_GM), F32),
    ("vn", (TOK, D_GM), BF16),
)
_DECAY_FIELDS = (
    ("sz", (TOK, D_SSD), F32),
    ("dsx", (TOK, D_SSD), F32),
    ("xsb", (TOK, D_SSD), BF16),
    ("gl", (TOK, D_GM), F32),
    ("lhs", (N_CHUNKS * SSD_HEADS, CHUNK, CHUNK + D_STATE), BF16),
    ("xw", (TOK, D_SSD), BF16),
    ("bt", (N_CHUNKS * SSD_GROUPS, D_STATE, CHUNK), BF16),
    ("dec", (N_CHUNKS * SSD_GROUPS, SUBLANES, GROUP_W), F32),
)


def _project_stage(x_ref, mod_ref, w, xbc_scr, hb_scr, slot):
    tok = x_ref.shape[1]
    x = x_ref[0]
    mod = mod_ref[0, 0]
    sh1, sc1 = mod[0:1], mod[1:2]
    h = (x * _rms_scale(x)) * (w["n1g"][...] * (1.0 + sc1)) + sh1
    hb_scr[...] = h.astype(BF16)
    proj = _dot(hb_scr[...], w["win"][...])
    slot["z"][...] = proj[:, :D_SSD]
    xbc_scr[HIST:HIST + tok, :] = proj[:, D_SSD:D_SSD + CONV_DIM]
    u_raw = proj[:, D_SSD + CONV_DIM:D_SSD + CONV_DIM + D_GM]
    v_raw = proj[:, D_SSD + CONV_DIM + D_GM:]
    slot["dt"][...] = _dot_nt(w["wdtt"][...], hb_scr[...])
    acc = _causal_conv(xbc_scr[0:HIST + tok, :], w["cw"], w["cb"], SSD_CONV)
    xbc_scr[0:HIST, :] = xbc_scr[tok:tok + HIST, :]
    xbc = _silu(acc)
    slot["xs"][...] = xbc[:, :D_SSD]
    slot["bc"][...] = xbc[:, D_SSD:]
    slot["u"][...] = _gelu(u_raw)
    v = _gelu(v_raw)
    slot["vn"][...] = ((v * _rms_scale(v)) * w["vng"][...]).astype(BF16)


def _decay_stage(src, w, slot):
    tok = src["xs"].shape[0]
    n_chunks = tok // CHUNK
    bc_w = SSD_GROUPS * D_STATE

    dt_in = src["dt"][...] + w["dtb"][...]
    dt_t = jnp.maximum(dt_in, 0.0) + jnp.log1p(jnp.exp(-jnp.abs(dt_in)))
    dta_t = dt_t * (-jnp.exp(w["alog"][...]))
    log2_dt_t = jnp.log(dt_t) * LOG2E

    li = lax.broadcasted_iota(jnp.int32, (CHUNK, CHUNK), 0)
    si = lax.broadcasted_iota(jnp.int32, (CHUNK, CHUNK), 1)
    causal = li >= si
    triu_b = jnp.where(li <= si, 1.0, 0.0).astype(BF16)
    low_half = si[0:1] < SSD_HEAD_DIM

    for c in range(tok // CHUNK):
        r0 = c * CHUNK
        hi, mid, lo = _split3(dta_t[:, r0:r0 + CHUNK])
        a2_t = (_dot(hi, triu_b) + _dot(mid, triu_b) + _dot(lo, triu_b)) * LOG2E
        dt_c = dt_t[:, r0:r0 + CHUNK]
        row_t = a2_t - log2_dt_t[:, r0:r0 + CHUNK]
        w_t = dt_c * jnp.exp2(a2_t[:, CHUNK - 1:CHUNK] - a2_t)
        rows = slice(r0, r0 + CHUNK)
        xs_c = src["xs"][rows, :]
        slot["sz"][rows, :] = _silu(src["z"][rows, :])
        slot["xsb"][rows, :] = xs_c.astype(BF16)
        slot["dsx"][rows, :] = w["dsk"][...] * xs_c
        for g in range(SSD_GROUPS):
            b_blk = src["bc"][r0:r0 + CHUNK, g * D_STATE:(g + 1) * D_STATE]
            c_blk = src["bc"][r0:r0 + CHUNK,
                              bc_w + g * D_STATE:bc_w + (g + 1) * D_STATE]
            c_b = c_blk.astype(BF16)
            cb_b = _dot_nt(c_b, b_blk.astype(BF16)).astype(BF16)
            slot["bt"][c * SSD_GROUPS + g] = b_blk.T.astype(BF16)
            w_parts, dec_parts = [], []
            for j in range(HEADS_PER_GROUP // 2):
                w_pair, e_pair = [], []
                for hd in (g * HEADS_PER_GROUP + 2 * j,
                           g * HEADS_PER_GROUP + 2 * j + 1):
                    col = jnp.broadcast_to(a2_t[hd:hd + 1, :], (CHUNK, CHUNK)).T
                    seg = jnp.where(causal, col - row_t[hd:hd + 1, :], -jnp.inf)
                    e_col = jnp.exp2(col)
                    slot["lhs"][c * SSD_HEADS + hd] = jnp.concatenate(
                        [cb_b * jnp.exp2(seg).astype(BF16),
                         c_b * e_col.astype(BF16)], axis=1)
                    w_pair.append(jnp.broadcast_to(
                        w_t[hd:hd + 1, :], (SSD_HEAD_DIM, CHUNK)))
                    e_pair.append(e_col[CHUNK - 1:CHUNK, :])
                w_parts.append(jnp.concatenate(w_pair, axis=0).T)
                dec_parts.append(jnp.where(low_half, e_pair[0], e_pair[1]))
            w_exp = jnp.concatenate(w_parts, axis=1)
            slot["xw"][r0:r0 + CHUNK, g * GROUP_W:(g + 1) * GROUP_W] = (
                xs_c[:, g * GROUP_W:(g + 1) * GROUP_W] * w_exp
            ).astype(BF16)
            slot["dec"][c * SSD_GROUPS + g, 0:1, :] = jnp.concatenate(
                dec_parts, axis=1)

    vn = src["vn"][...]
    sv_heads = []
    for hd in range(GM_HEADS):
        ws_m = jnp.where(causal, w["ws"][hd], 0.0).astype(BF16)
        v_h = jnp.concatenate(
            [vn[c * CHUNK:(c + 1) * CHUNK,
                hd * GM_HEAD_DIM:(hd + 1) * GM_HEAD_DIM]
             for c in range(n_chunks)], axis=1)
        sv_heads.append(_dot(ws_m, v_h))
    sv = jnp.concatenate(
        [jnp.concatenate([s[:, c * CHUNK:(c + 1) * CHUNK] for s in sv_heads],
                         axis=1) + w["bsx"][...]
         for c in range(n_chunks)], axis=0)
    slot["gl"][...] = src["u"][...] * sv


def _scan_out_stage(slot, x_ref, mod_ref, w, state_scr, mix_scr, o_ref):
    tok = slot["sz"].shape[0]
    n_chunks = tok // CHUNK
    low_half = lax.broadcasted_iota(jnp.int32, (CHUNK, CHUNK), 1) < SSD_HEAD_DIM

    y_rows = []
    for c in range(n_chunks):
        r0 = c * CHUNK
        y_parts = []
        for g in range(SSD_GROUPS):
            state = state_scr[g]
            rhs_g = jnp.concatenate(
                [slot["xsb"][r0:r0 + CHUNK, g * GROUP_W:(g + 1) * GROUP_W],
                 state.astype(BF16)], axis=0)
            for j in range(HEADS_PER_GROUP // 2):
                rhs = rhs_g[:, j * LANES:(j + 1) * LANES]
                hd = c * SSD_HEADS + g * HEADS_PER_GROUP + 2 * j
                y_even = _dot(slot["lhs"][hd], rhs)
                y_odd = _dot(slot["lhs"][hd + 1], rhs)
                y_parts.append(jnp.where(low_half, y_even, y_odd))
            dec = slot["dec"][c * SSD_GROUPS + g, 0:1, :]
            state_scr[g] = state * dec + _dot(
                slot["bt"][c * SSD_GROUPS + g],
                slot["xw"][r0:r0 + CHUNK, g * GROUP_W:(g + 1) * GROUP_W])
        y_rows.append(jnp.concatenate(y_parts, axis=1))
    y_ssd = jnp.concatenate(y_rows, axis=0)

    yg_all = (y_ssd + slot["dsx"][...]) * slot["sz"][...]
    for g in range(SSD_GROUPS):
        yg = yg_all[:, g * GROUP_W:(g + 1) * GROUP_W]
        mix_scr[:, g * GROUP_W:(g + 1) * GROUP_W] = (
            (yg * _rms_scale(yg)) * w["ssdg"][:, g * GROUP_W:(g + 1) * GROUP_W]
        ).astype(BF16)
    gl = slot["gl"][...]
    mix_scr[:, D_SSD:] = ((gl * _rms_scale(gl)) * w["gog"][...]).astype(BF16)
    out = _dot(mix_scr[...], w["wo"][...])
    g1 = mod_ref[0, 0][2:3]
    o_ref[0] = x_ref[0] + g1 * out


_MIXER_CONSTS = ("n1g", "win", "wdtt", "cw", "cb", "dtb",
                 "alog", "dsk", "ssdg", "vng", "ws", "bsx", "gog", "wo")


def _mixer_kernel(*refs, tiles_per_seq):
    n_const = len(_MIXER_CONSTS)
    x_ref, x_out_ref, mod_in_ref, mod_out_ref = refs[:4]
    w = {name: ref.at[0] for name, ref in zip(_MIXER_CONSTS, refs[4:4 + n_const])}
    o_ref = refs[4 + n_const]
    xbc_scr, state_scr, hb_scr, mix_scr = refs[5 + n_const:9 + n_const]
    n_proj = 2 * len(_PROJ_FIELDS)
    proj_slots = _slot_dicts(refs[9 + n_const:9 + n_const + n_proj], _PROJ_FIELDS)
    decay_slots = _slot_dicts(refs[9 + n_const + n_proj:], _DECAY_FIELDS)
    k = pl.program_id(0)

    @pl.when(k == 0)
    def _():
        _zero_slot(proj_slots[1])
        _zero_slot(decay_slots[0])

    @pl.when(k % tiles_per_seq == 0)
    def _():
        xbc_scr[0:HIST, :] = jnp.zeros((HIST, CONV_DIM), F32)

    @pl.when(jnp.logical_or(k <= 1, (k - 2) % tiles_per_seq == 0))
    def _():
        state_scr[...] = jnp.zeros(state_scr.shape, F32)

    def step(p):
        _scan_out_stage(decay_slots[p], x_out_ref, mod_out_ref, w, state_scr,
                        mix_scr, o_ref)
        _decay_stage(proj_slots[1 - p], w, decay_slots[1 - p])
        _project_stage(x_ref, mod_in_ref, w, xbc_scr, hb_scr, proj_slots[p])

    @pl.when(k % 2 == 0)
    def _():
        step(0)

    @pl.when(k % 2 == 1)
    def _():
        step(1)


def _mixer(x, mod, p, l):
    bsz, seq, _ = x.shape
    tps = seq // TOK
    n_tiles = bsz * tps
    consts = [p[name] for name in _MIXER_CONSTS]

    def in_tile(k):
        t = jnp.minimum(k, n_tiles - 1)
        return t // tps, t % tps

    def out_tile(k):
        t = jnp.maximum(k - 2, 0)
        return t // tps, t % tps

    return pl.pallas_call(
        functools.partial(_mixer_kernel, tiles_per_seq=tps),
        grid=(n_tiles + 2,),
        in_specs=[
            pl.BlockSpec((1, TOK, D_MODEL), lambda k: (*in_tile(k), 0)),
            pl.BlockSpec((1, TOK, D_MODEL), lambda k: (*out_tile(k), 0)),
            pl.BlockSpec((1, 1, N_MOD, D_MODEL),
                         lambda k: (l, in_tile(k)[0], 0, 0)),
            pl.BlockSpec((1, 1, N_MOD, D_MODEL),
                         lambda k: (l, out_tile(k)[0], 0, 0)),
        ] + [_layer_spec(a, l) for a in consts],
        out_specs=pl.BlockSpec((1, TOK, D_MODEL), lambda k: (*out_tile(k), 0)),
        out_shape=jax.ShapeDtypeStruct(x.shape, F32),
        scratch_shapes=[
            pltpu.VMEM((HIST + TOK, CONV_DIM), F32),
            pltpu.VMEM((SSD_GROUPS, D_STATE, GROUP_W), F32),
            pltpu.VMEM((TOK, D_MODEL), BF16),
            pltpu.VMEM((TOK, D_SSD + D_GM), BF16),
        ] + 2 * _slot_shapes(_PROJ_FIELDS) + 2 * _slot_shapes(_DECAY_FIELDS),
        compiler_params=pltpu.CompilerParams(
            dimension_semantics=("arbitrary",),
            vmem_limit_bytes=VMEM_LIMIT),
        name="mixer",
    )(x, x, mod, mod, *consts)


def _ffn_up_stage(x_ref, mod_ref, w, gate_scr, hb_scr, slot):
    tok = x_ref.shape[1]
    x = x_ref[0]
    mod = mod_ref[0, 0]
    sh2, sc2 = mod[3:4], mod[4:5]
    h = (x * _rms_scale(x)) * (w["n2g"][...] * (1.0 + sc2)) + sh2
    hb_scr[...] = h.astype(BF16)
    slot["x"][...] = x
    gate_scr[HIST:HIST + tok, :] = _dot(hb_scr[...], w["wg"][...])
    val = _dot(hb_scr[...], w["wval"][...])
    acc = _causal_conv(gate_scr[0:HIST + tok, :], w["fcw"], w["fcb"], FF_CONV)
    gate_scr[0:HIST, :] = gate_scr[tok:tok + HIST, :]
    slot["act"][...] = (_silu(acc) * val).astype(BF16)


def _ffn_down_stage(slot, mod_ref, w, dst_ref):
    g2 = mod_ref[0, 0][5:6]
    dst_ref[...] = slot["x"][...] + g2 * _dot(slot["act"][...], w["wd"][...])


def _final_norm_stage(slot, w, o_ref):
    xo = slot["xo"][...]
    o_ref[0] = (xo * _rms_scale(xo)) * w["fg"][...]


_FFN_CONSTS = ("n2g", "wg", "wval", "fcw", "fcb", "wd", "fg")
_FFN_FIELDS = (("x", (FFN_TOK, D_MODEL), F32), ("act", (FFN_TOK, D_FF), BF16))
_FFN_OUT_FIELDS = (("xo", (FFN_TOK, D_MODEL), F32),)


def _ffn_kernel(*refs, tiles_per_seq, final):
    n_const = len(_FFN_CONSTS)
    x_ref, mod_in_ref, mod_down_ref = refs[:3]
    w = {name: ref.at[0] for name, ref in zip(_FFN_CONSTS, refs[3:3 + n_const])}
    o_ref = refs[3 + n_const]
    gate_scr, hb_scr = refs[4 + n_const:6 + n_const]
    n_slot = 2 * len(_FFN_FIELDS)
    slots = _slot_dicts(refs[6 + n_const:6 + n_const + n_slot], _FFN_FIELDS)
    out_slots = _slot_dicts(refs[6 + n_const + n_slot:], _FFN_OUT_FIELDS)
    k = pl.program_id(0)

    @pl.when(k == 0)
    def _():
        _zero_slot(slots[1])
        if final:
            _zero_slot(out_slots[0])

    @pl.when(k % tiles_per_seq == 0)
    def _():
        gate_scr[0:HIST, :] = jnp.zeros((HIST, D_FF), F32)

    def step(p):
        if final:
            _final_norm_stage(out_slots[p], w, o_ref)
            _ffn_down_stage(slots[1 - p], mod_down_ref, w, out_slots[1 - p]["xo"])
        else:
            _ffn_down_stage(slots[1 - p], mod_down_ref, w, o_ref.at[0])
        _ffn_up_stage(x_ref, mod_in_ref, w, gate_scr, hb_scr, slots[p])

    @pl.when(k % 2 == 0)
    def _():
        step(0)

    @pl.when(k % 2 == 1)
    def _():
        step(1)


def _ffn(x, mod, p, l, final):
    bsz, seq, _ = x.shape
    tps = seq // FFN_TOK
    n_tiles = bsz * tps
    lag = 2 if final else 1
    half_up = (D_MODEL, D_FF)
    operands = [(p["n2g"], _layer_spec(p["n2g"], l)),
                (p["wup"], _layer_spec(p["wup"], l, half_up, (0, 0))),
                (p["wup"], _layer_spec(p["wup"], l, half_up, (0, 1))),
                (p["fcw"], _layer_spec(p["fcw"], l)),
                (p["fcb"], _layer_spec(p["fcb"], l)),
                (p["wd"], _layer_spec(p["wd"], l)),
                (p["fg"], _layer_spec(p["fg"], 0))]

    def tile(k, behind):
        t = jnp.clip(k - behind, 0, n_tiles - 1)
        return t // tps, t % tps

    return pl.pallas_call(
        functools.partial(_ffn_kernel, tiles_per_seq=tps, final=final),
        grid=(n_tiles + lag,),
        in_specs=[
            pl.BlockSpec((1, FFN_TOK, D_MODEL), lambda k: (*tile(k, 0), 0)),
            pl.BlockSpec((1, 1, N_MOD, D_MODEL),
                         lambda k: (l, tile(k, 0)[0], 0, 0)),
            pl.BlockSpec((1, 1, N_MOD, D_MODEL),
                         lambda k: (l, tile(k, 1)[0], 0, 0)),
        ] + [spec for _, spec in operands],
        out_specs=pl.BlockSpec((1, FFN_TOK, D_MODEL),
                               lambda k: (*tile(k, lag), 0)),
        out_shape=jax.ShapeDtypeStruct(x.shape, F32),
        scratch_shapes=[pltpu.VMEM((HIST + FFN_TOK, D_FF), F32),
                        pltpu.VMEM((FFN_TOK, D_MODEL), BF16)]
        + 2 * _slot_shapes(_FFN_FIELDS)
        + (2 * _slot_shapes(_FFN_OUT_FIELDS) if final else []),
        compiler_params=pltpu.CompilerParams(
            dimension_semantics=("arbitrary",),
            vmem_limit_bytes=VMEM_LIMIT),
        name="ffn_final" if final else "ffn",
    )(x, mod, mod, *[a for a, _ in operands])


def _prep_params(norm1_g, norm2_g, w_in, ssd_conv_w, ssd_conv_b, ssd_dt_bias,
                 ssd_a_log, ssd_d, ssd_norm_g, gm_vnorm_g, gm_ws, gm_bs,
                 gm_out_g, w_out, ff_up, ff_conv_w, ff_conv_b, ff_down, final_g):
    o_dt = D_SSD + CONV_DIM
    w_in_t = jnp.swapaxes(w_in, 1, 2)
    row = lambda a: a[:, None, :]
    col = lambda a: a[:, :, None]
    return {
        "n1g": row(norm1_g),
        "win": _regroup_win(w_in_t),
        "wdtt": w_in_t[:, o_dt:o_dt + SSD_HEADS, :].astype(BF16),
        "cw": ssd_conv_w,
        "cb": row(ssd_conv_b),
        "dtb": col(ssd_dt_bias),
        "alog": col(ssd_a_log),
        "dsk": row(jnp.repeat(ssd_d, SSD_HEAD_DIM, axis=1)),
        "ssdg": row(ssd_norm_g),
        "vng": row(gm_vnorm_g),
        "ws": gm_ws,
        "bsx": jnp.repeat(jnp.swapaxes(gm_bs, 1, 2), GM_HEAD_DIM, axis=2),
        "gog": row(gm_out_g),
        "wo": w_out.astype(BF16),
        "n2g": row(norm2_g),
        "wup": ff_up.astype(BF16),
        "fcw": ff_conv_w,
        "fcb": row(ff_conv_b),
        "wd": ff_down.astype(BF16),
        "fg": final_g.reshape(1, 1, D_MODEL),
    }


def kernel(x, c, ada_w, ada_b, norm1_g, norm2_g, w_in, ssd_conv_w, ssd_conv_b, ssd_dt_bias, ssd_a_log, ssd_d, ssd_norm_g, gm_vnorm_g, gm_ws, gm_bs, gm_out_g, w_out, ff_up, ff_conv_w, ff_conv_b, ff_down, final_g):
    depth = ada_w.shape[0]
    bsz = x.shape[0]
    mod = _ada(c, ada_w, ada_b).reshape(depth, bsz, N_MOD, D_MODEL)
    p = _prep_params(norm1_g, norm2_g, w_in, ssd_conv_w, ssd_conv_b, ssd_dt_bias,
                     ssd_a_log, ssd_d, ssd_norm_g, gm_vnorm_g, gm_ws, gm_bs,
                     gm_out_g, w_out, ff_up, ff_conv_w, ff_conv_b, ff_down,
                     final_g)
    for l in range(depth):
        x = _mixer(x, mod, p, l)
        x = _ffn(x, mod, p, l, final=(l == depth - 1))
    return x
```
